```python
import math
import jax
import jax.numpy as jnp
from jax import lax
import numpy as np

D_MODEL = 1024
BATCH = 8
SEQ = 2048
DEPTH = 4
DEC_BATCH = 128
DEC_SEQ = 4
PAST_LEN = 16384
PAGE_SIZE = 128

H_A = 8
DK_A = 128
DV_A = 128
KEY_A = H_A * DK_A
VAL_A = H_A * DV_A
CONV_W = 4
CONV_DIM = 2 * KEY_A + VAL_A
H_B = 8
EXP_B = 128
DH_B = D_MODEL // H_B
FORGET_DIM = H_B * EXP_B
VAL_B = H_B * DH_B
F_MIN = 1e-30
CHUNK = 64
N_GROUPS = 4
EXPERTS_PER_GROUP = 8
N_EXPERTS = N_GROUPS * EXPERTS_PER_GROUP
TOP_K = 2
D_EXPERT = 512
MOE_BLOCK = 64
ALPHA = (2 * DEPTH) ** 0.25
BETA_INIT = (8 * DEPTH) ** -0.25
LN_EPS = 1e-5
RMS_EPS = 1e-6
OFF_QKV_A = 0
OFF_G_A = OFF_QKV_A + CONV_DIM
OFF_DECAY_A = OFF_G_A + VAL_A
OFF_BETA_A = OFF_DECAY_A + H_A
OFF_Q_B = OFF_BETA_A + H_A
OFF_F_B = OFF_Q_B + FORGET_DIM
OFF_I_B = OFF_F_B + FORGET_DIM
OFF_G_B = OFF_I_B + VAL_B
OFF_MERGE = OFF_G_B + VAL_B
N_IN = OFF_MERGE + 2 * D_MODEL

kernel_name = 'hybrid_gdn_hgrn2_hmoe_deepnorm_step'


def layer_norm(x, g, b):
    xf = x.astype(jnp.float32)
    mu = jnp.mean(xf, -1, keepdims=True)
    var = jnp.mean(jnp.square(xf - mu), -1, keepdims=True)
    y = (xf - mu) * lax.rsqrt(var + LN_EPS)
    return (y * g.astype(jnp.float32) + b.astype(jnp.float32)).astype(x.dtype)


def l2_normalize(x):
    return x * lax.rsqrt(jnp.sum(jnp.square(x), -1, keepdims=True) + RMS_EPS)


def gated_rms_norm(o, gate, gain, dtype):
    B, T, H, d = o.shape
    y = o * lax.rsqrt(jnp.mean(jnp.square(o), -1, keepdims=True) + RMS_EPS) * gain.astype(jnp.float32)
    y = y.reshape(B, T, H * d) * jax.nn.silu(gate.astype(jnp.float32))
    return y.astype(dtype)


def masked_exp(diff, mask):
    return jnp.where(mask, jnp.exp(jnp.where(mask, diff, 0.0)), 0.0)


def to_chunks(x, C):
    B, T, H = x.shape[:3]
    x = x.reshape((B, T // C, C, H) + x.shape[3:])
    return jnp.moveaxis(x, (1, 3), (0, 2))


def from_chunks(x):
    N, B, H, C = x.shape[:4]
    x = jnp.moveaxis(x, (0, 2), (1, 3))
    return x.reshape((B, N * C, H) + x.shape[4:])


def gated_delta_rule(q, k, v, beta, g, s0):
    f32 = jnp.float32
    T = q.shape[1]
    C = math.gcd(T, CHUNK)
    causal = jnp.tril(jnp.ones((C, C), bool))
    strict = jnp.tril(jnp.ones((C, C), bool), -1)

    def step(S, inp):
        qc, kc, vc, bc, gch = inp
        gcum = jnp.cumsum(gch, -1)
        decay = masked_exp(gcum[..., :, None] - gcum[..., None, :], causal)
        kb = kc * bc[..., None]
        m = jnp.where(strict, jnp.einsum('bhid,bhjd->bhij', kb, kc) * decay, 0.0)
        rhs = jnp.concatenate([kb * jnp.exp(gcum)[..., None], vc * bc[..., None]], -1)
        sol = lax.linalg.triangular_solve(m, rhs, left_side=True, lower=True, unit_diagonal=True)
        w, u = sol[..., :DK_A], sol[..., DK_A:]
        v_new = u - jnp.einsum('bhcd,bhde->bhce', w, S)
        o = (jnp.einsum('bhcd,bhde->bhce', qc * jnp.exp(gcum)[..., None], S)
             + jnp.einsum('bhij,bhje->bhie', jnp.einsum('bhid,bhjd->bhij', qc, kc) * decay, v_new))
        g_last = gcum[..., -1:]
        S = (jnp.exp(g_last)[..., None] * S
             + jnp.einsum('bhcd,bhce->bhde', kc * jnp.exp(g_last - gcum)[..., None], v_new))
        return S, o

    xs = (to_chunks(q.astype(f32), C), to_chunks(k.astype(f32), C), to_chunks(v.astype(f32), C),
          to_chunks(beta.astype(f32), C), to_chunks(g.astype(f32), C))
    s_fin, o = lax.scan(step, s0.astype(f32), xs)
    return from_chunks(o), s_fin


def hgrn2_recurrence(q, k, i, log_f, s0):
    f32 = jnp.float32
    T = q.shape[1]
    C = math.gcd(T, CHUNK)
    causal = jnp.tril(jnp.ones((C, C), bool))[:, :, None]

    def step(S, inp):
        qc, kc, ic, lf = inp
        b = jnp.cumsum(lf, -2)
        dec = masked_exp(b[..., :, None, :] - b[..., None, :, :], causal)
        a = jnp.einsum('bhtd,bhtsd,bhsd->bhts', qc, dec, kc)
        o = jnp.einsum('bhtd,bhde->bhte', qc * jnp.exp(b), S) + jnp.einsum('bhts,bhse->bhte', a, ic)
        b_last = b[..., -1:, :]
        S = (jnp.swapaxes(jnp.exp(b_last), -1, -2) * S
             + jnp.einsum('bhsd,bhse->bhde', kc * jnp.exp(b_last - b), ic))
        return S, o

    xs = (to_chunks(q.astype(f32), C), to_chunks(k.astype(f32), C), to_chunks(i.astype(f32), C),
          to_chunks(log_f.astype(f32), C))
    s_fin, o = lax.scan(step, s0.astype(f32), xs)
    return from_chunks(o), s_fin


def token_mixers(h, conv_buf, s_delta, s_hgrn, lb, w_in, conv_w, a_log, dt_bias, norm_a, norm_b,
                 w_br_a, w_br_b, w_o):
    f32 = jnp.float32
    B, T, _ = h.shape
    proj = h @ w_in
    qkv = proj[..., OFF_QKV_A:OFF_G_A]
    xpad = jnp.concatenate([conv_buf.astype(qkv.dtype), qkv], axis=1)
    new_buf = xpad[:, -(CONV_W - 1):]
    qkv = lax.conv_general_dilated(xpad, conv_w.astype(xpad.dtype)[:, None, :], (1,), 'VALID',
                                   dimension_numbers=('NWC', 'WIO', 'NWC'),
                                   feature_group_count=CONV_DIM)
    qkv = jax.nn.silu(qkv.astype(f32))
    q_a = l2_normalize(qkv[..., :KEY_A].reshape(B, T, H_A, DK_A)) * DK_A ** -0.5
    k_a = l2_normalize(qkv[..., KEY_A:2 * KEY_A].reshape(B, T, H_A, DK_A))
    v_a = qkv[..., 2 * KEY_A:].reshape(B, T, H_A, DV_A)
    beta = jax.nn.sigmoid(proj[..., OFF_BETA_A:OFF_Q_B].astype(f32))
    g = -jnp.exp(a_log.astype(f32)) * jax.nn.softplus(
        proj[..., OFF_DECAY_A:OFF_BETA_A].astype(f32) + dt_bias.astype(f32))
    o_a, s_delta_new = gated_delta_rule(q_a, k_a, v_a, beta, g, s_delta)
    o_a = gated_rms_norm(o_a, proj[..., OFF_G_A:OFF_DECAY_A], norm_a, h.dtype)
    q_b = jax.nn.silu(proj[..., OFF_Q_B:OFF_F_B].astype(f32)).reshape(B, T, H_B, EXP_B)
    z = proj[..., OFF_F_B:OFF_I_B].astype(f32).reshape(B, T, H_B, EXP_B)
    lbh = lb.reshape(H_B, EXP_B)
    f = lbh + (1.0 - lbh) * jax.nn.sigmoid(z)
    log_f = jnp.log(jnp.maximum(f, F_MIN))
    k_b = (1.0 - lbh) * jax.nn.sigmoid(-z)
    i_b = proj[..., OFF_I_B:OFF_G_B].astype(f32).reshape(B, T, H_B, DH_B)
    o_b, s_hgrn_new = hgrn2_recurrence(q_b, k_b, i_b, log_f, s_hgrn)
    o_b = gated_rms_norm(o_b, proj[..., OFF_G_B:OFF_MERGE], norm_b, h.dtype)
    gates = jax.nn.sigmoid(proj[..., OFF_MERGE:])
    merged = gates[..., :D_MODEL] * (o_a @ w_br_a) + gates[..., D_MODEL:] * (o_b @ w_br_b)
    return (merged @ w_o, new_buf, s_delta_new.astype(s_delta.dtype), s_hgrn_new.astype(s_hgrn.dtype))


def hier_moe(h, w_grp, b_grp, w_rt, b_rt, w_gate_e, w_up_e, w_down_e):
    f32 = jnp.float32
    n = h.shape[0]
    hf = h.astype(f32)
    grp_logits = hf @ w_grp.astype(f32) + b_grp.astype(f32)
    grp = jnp.argmax(grp_logits, -1).astype(jnp.int32)
    p_grp = jnp.max(jax.nn.softmax(grp_logits, -1), -1, keepdims=True)
    exp_logits = (hf @ w_rt.astype(f32) + b_rt.astype(f32)).reshape(n, N_GROUPS, EXPERTS_PER_GROUP)
    in_grp = jnp.einsum('ng,nge->ne', jax.nn.one_hot(grp, N_GROUPS, dtype=f32), exp_logits)
    top_v, top_i = lax.top_k(in_grp, TOP_K)
    comb = jax.nn.softmax(top_v, -1) * p_grp
    eid = (grp[:, None] * EXPERTS_PER_GROUP + top_i.astype(jnp.int32)).reshape(-1)
    wts = comb.reshape(-1)
    tok = jnp.repeat(jnp.arange(n, dtype=jnp.int32), TOP_K)
    n_slots = n * TOP_K
    n_blocks = (n_slots + N_EXPERTS * (MOE_BLOCK - 1)) // MOE_BLOCK
    order = jnp.argsort(eid)
    e_sorted = eid[order]
    counts = jnp.zeros((N_EXPERTS,), jnp.int32).at[eid].add(1)
    padded = (counts + MOE_BLOCK - 1) // MOE_BLOCK * MOE_BLOCK
    pad_end = jnp.cumsum(padded)
    pad_start = pad_end - padded
    seg_start = jnp.cumsum(counts) - counts
    dest = pad_start[e_sorted] + jnp.arange(n_slots, dtype=jnp.int32) - seg_start[e_sorted]
    slot_tok = jnp.full((n_blocks * MOE_BLOCK,), n, jnp.int32).at[dest].set(tok[order])
    slot_w = jnp.zeros((n_blocks * MOE_BLOCK,), f32).at[dest].set(wts[order])
    block_start = jnp.arange(n_blocks, dtype=jnp.int32) * MOE_BLOCK
    block_e = jnp.minimum(jnp.sum(pad_end[None, :] <= block_start[:, None], axis=1), N_EXPERTS - 1)
    h_pad = jnp.concatenate([h, jnp.zeros((1, D_MODEL), h.dtype)], axis=0)
    xs = h_pad[slot_tok].reshape(n_blocks, MOE_BLOCK, D_MODEL)

    def expert_block(args):
        xb, e = args
        hid = jax.nn.silu(xb @ w_gate_e[e]) * (xb @ w_up_e[e])
        return hid @ w_down_e[e]

    ys = lax.map(expert_block, (xs, block_e)).reshape(n_blocks * MOE_BLOCK, D_MODEL)
    out = jnp.zeros((n + 1, D_MODEL), ys.dtype).at[slot_tok].add(ys * slot_w[:, None].astype(ys.dtype))
    return out[:n].astype(h.dtype)


def trunk(x, c, st_conv, st_delta, st_hgrn, ln0_g, ln0_b, w_ada, b_ada, w_in, conv_w, a_log, dt_bias,
          norm_a, lb_param, norm_b, w_br_a, w_br_b, w_o, ln1_g, ln1_b, w_grp, b_grp, w_rt, b_rt,
          w_gate_e, w_up_e, w_down_e, ln2_g, ln2_b):
    B, T, _ = x.shape
    x = layer_norm(x, ln0_g, ln0_b)
    p = jax.nn.softmax(lb_param.astype(jnp.float32), axis=0)
    lower_bounds = jnp.cumsum(p, axis=0) - p[:1]
    c_act = jax.nn.silu(c)
    convs, deltas, hgrns = [], [], []
    for l in range(DEPTH):
        mod = (c_act @ w_ada[l] + b_ada[l])[:, None, :]
        sh1, sc1, gt1, sh2, sc2, gt2 = jnp.split(mod, 6, axis=-1)
        h = x * (1 + sc1) + sh1
        y, cb, sd, shg = token_mixers(h, st_conv[l], st_delta[l], st_hgrn[l], lower_bounds[l], w_in[l],
                                      conv_w[l], a_log[l], dt_bias[l], norm_a[l], norm_b[l],
                                      w_br_a[l], w_br_b[l], w_o[l])
        x = layer_norm(ALPHA * x + (1 + gt1) * y, ln1_g[l], ln1_b[l])
        h = x * (1 + sc2) + sh2
        y = hier_moe(h.reshape(B * T, D_MODEL), w_grp[l], b_grp[l], w_rt[l], b_rt[l],
                     w_gate_e[l], w_up_e[l], w_down_e[l]).reshape(B, T, D_MODEL)
        x = layer_norm(ALPHA * x + (1 + gt2) * y, ln2_g[l], ln2_b[l])
        convs.append(cb)
        deltas.append(sd)
        hgrns.append(shg)
    return x, jnp.stack(convs), jnp.stack(deltas), jnp.stack(hgrns)


def setup_inputs(seed: int = 0) -> dict:
    key = jax.random.key(seed)
    ks = jax.random.split(key, 40)
    f32 = jnp.float32

    def nrm(k, shape, s):
        return jax.random.normal(k, shape, f32) * s

    dt = jnp.exp(jax.random.uniform(ks[14], (DEPTH, H_A), f32, math.log(1e-3), math.log(1e-1)))
    return {
        'x_prompt': nrm(ks[0], (BATCH, SEQ, D_MODEL), 1.0),
        'x_sample': nrm(ks[1], (DEC_BATCH, DEC_SEQ, D_MODEL), 1.0),
        'state_conv': nrm(ks[2], (DEPTH, DEC_BATCH, CONV_W - 1, CONV_DIM), 1.0),
        'state_delta': nrm(ks[3], (DEPTH, DEC_BATCH, H_A, DK_A, DV_A), 0.1),
        'state_hgrn': nrm(ks[4], (DEPTH, DEC_BATCH, H_B, EXP_B, DH_B), 0.5),
        'c_prompt': nrm(ks[5], (BATCH, D_MODEL), 1.0),
        'c_sample': nrm(ks[6], (DEC_BATCH, D_MODEL), 1.0),
        'ln0_g': 1.0 + nrm(ks[7], (D_MODEL,), 0.02),
        'ln0_b': nrm(ks[8], (D_MODEL,), 0.02),
        'w_ada': nrm(ks[9], (DEPTH, D_MODEL, 6 * D_MODEL), 0.1 * D_MODEL ** -0.5),
        'b_ada': nrm(ks[10], (DEPTH, 6 * D_MODEL), 0.01),
        'w_in': nrm(ks[11], (DEPTH, D_MODEL, N_IN), D_MODEL ** -0.5),
        'conv_w': nrm(ks[12], (DEPTH, CONV_W, CONV_DIM), CONV_W ** -0.5),
        'a_log': jnp.log(jax.random.uniform(ks[13], (DEPTH, H_A), f32, 1.0, 16.0)),
        'dt_bias': dt + jnp.log(-jnp.expm1(-dt)),
        'norm_a': 1.0 + nrm(ks[15], (DEPTH, DV_A), 0.02),
        'lb_param': nrm(ks[16], (DEPTH, FORGET_DIM), 0.1),
        'norm_b': 1.0 + nrm(ks[17], (DEPTH, DH_B), 0.02),
        'w_br_a': nrm(ks[18], (DEPTH, VAL_A, D_MODEL), BETA_INIT * VAL_A ** -0.5),
        'w_br_b': nrm(ks[19], (DEPTH, VAL_B, D_MODEL), BETA_INIT * VAL_B ** -0.5),
        'w_o': nrm(ks[20], (DEPTH, D_MODEL, D_MODEL), BETA_INIT * D_MODEL ** -0.5),
        'ln1_g': 1.0 + nrm(ks[21], (DEPTH, D_MODEL), 0.02),
        'ln1_b': nrm(ks[22], (DEPTH, D_MODEL), 0.02),
        'w_grp': nrm(ks[23], (DEPTH, D_MODEL, N_GROUPS), D_MODEL ** -0.5),
        'b_grp': nrm(ks[24], (DEPTH, N_GROUPS), 0.01),
        'w_rt': nrm(ks[25], (DEPTH, D_MODEL, N_EXPERTS), D_MODEL ** -0.5),
        'b_rt': nrm(ks[26], (DEPTH, N_EXPERTS), 0.01),
        'w_gate_e': nrm(ks[27], (DEPTH, N_EXPERTS, D_MODEL, D_EXPERT), D_MODEL ** -0.5),
        'w_up_e': nrm(ks[28], (DEPTH, N_EXPERTS, D_MODEL, D_EXPERT), D_MODEL ** -0.5),
        'w_down_e': nrm(ks[29], (DEPTH, N_EXPERTS, D_EXPERT, D_MODEL), BETA_INIT * D_EXPERT ** -0.5),
        'ln2_g': 1.0 + nrm(ks[30], (DEPTH, D_MODEL), 0.02),
        'ln2_b': nrm(ks[31], (DEPTH, D_MODEL), 0.02),
    }


def reference(x_prompt, x_sample, state_conv, state_delta, state_hgrn, c_prompt, c_sample,
              ln0_g, ln0_b, w_ada, b_ada, w_in, conv_w, a_log, dt_bias, norm_a, lb_param, norm_b,
              w_br_a, w_br_b, w_o, ln1_g, ln1_b, w_grp, b_grp, w_rt, b_rt, w_gate_e, w_up_e, w_down_e,
              ln2_g, ln2_b):
    weights = (ln0_g, ln0_b, w_ada, b_ada, w_in, conv_w, a_log, dt_bias, norm_a, lb_param, norm_b,
               w_br_a, w_br_b, w_o, ln1_g, ln1_b, w_grp, b_grp, w_rt, b_rt, w_gate_e, w_up_e, w_down_e,
               ln2_g, ln2_b)
    bp = x_prompt.shape[0]
    zero_conv = jnp.zeros((DEPTH, bp, CONV_W - 1, CONV_DIM), x_prompt.dtype)
    zero_delta = jnp.zeros((DEPTH, bp, H_A, DK_A, DV_A), x_prompt.dtype)
    zero_hgrn = jnp.zeros((DEPTH, bp, H_B, EXP_B, DH_B), x_prompt.dtype)
    y_prompt, conv_p, delta_p, hgrn_p = trunk(x_prompt, c_prompt, zero_conv, zero_delta, zero_hgrn, *weights)
    y_sample, conv_s, delta_s, hgrn_s = trunk(x_sample, c_sample, state_conv, state_delta, state_hgrn, *weights)
    return (y_prompt, y_sample, conv_p, delta_p, hgrn_p, conv_s, delta_s, hgrn_s)
```

```python
import functools
import math

import jax
import jax.numpy as jnp
from jax import lax
from jax.experimental import pallas as pl
from jax.experimental.pallas import tpu as pltpu

F32 = jnp.float32
BF16 = jnp.bfloat16
I32 = jnp.int32

D = 1024
NH = 8
HD = 128
CONV_W = 4
CONV_DIM = 3 * D
N_GROUPS = 4
EPG = 8
N_EXP = N_GROUPS * EPG
D_EXP = 512
F_MIN = 1e-30
LN_EPS = 1e-5
RMS_EPS = 1e-6

C_QKV = 0
C_GA = 3 * D
C_QB = 4 * D
C_FB = 5 * D
C_IB = 6 * D
C_GB = 7 * D
C_MERGE = 8 * D
C_SMALL = 10 * D
NPROJ = 10 * D + 128

LANES = 128
SUBLANES = 8
VMEM_LIMIT = 48 * 1024 * 1024

HI = lax.Precision.HIGHEST


def _cparams(*sem):
    return pltpu.CompilerParams(dimension_semantics=sem, vmem_limit_bytes=VMEM_LIMIT)


def _dot(a, b, hi=False):
    if hi:
        return jnp.dot(a, b, preferred_element_type=F32, precision=HI)
    return jnp.dot(a.astype(BF16), b.astype(BF16), preferred_element_type=F32)


def _dot_nt(a, b, hi=False):
    dn = (((1,), (1,)), ((), ()))
    if hi:
        return lax.dot_general(a, b, dn, preferred_element_type=F32, precision=HI)
    return lax.dot_general(a.astype(BF16), b.astype(BF16), dn, preferred_element_type=F32)


def _dot_tn(a, b, hi=False):
    dn = (((0,), (0,)), ((), ()))
    if hi:
        return lax.dot_general(a, b, dn, preferred_element_type=F32, precision=HI)
    return lax.dot_general(a.astype(BF16), b.astype(BF16), dn, preferred_element_type=F32)


def _sigmoid(x):
    return 1.0 / (1.0 + jnp.exp(-x))


def _silu(x):
    return x * _sigmoid(x)


def _softplus(x):
    return jnp.maximum(x, 0.0) + jnp.log(1.0 + jnp.exp(-jnp.abs(x)))


def _layer_norm(x, g, b):
    mu = jnp.mean(x, -1, keepdims=True)
    xc = x - mu
    var = jnp.mean(xc * xc, -1, keepdims=True)
    return xc * lax.rsqrt(var + LN_EPS) * g + b


def _mod_kernel(c_ref, w_ref, b_ref, o_ref):
    ca = _silu(c_ref[...])
    o_ref[0] = _dot(ca, w_ref[0]) + b_ref[0]


def _modulation(c_all, w_ada, b_ada):
    depth = w_ada.shape[0]
    nb = c_all.shape[0]
    tn = 1536
    return pl.pallas_call(
        _mod_kernel,
        grid=(depth, 6 * D // tn),
        in_specs=[
            pl.BlockSpec((nb, D), lambda l, j: (0, 0)),
            pl.BlockSpec((1, D, tn), lambda l, j: (l, 0, j)),
            pl.BlockSpec((1, 1, tn), lambda l, j: (l, 0, j)),
        ],
        out_specs=pl.BlockSpec((1, nb, tn), lambda l, j: (l, 0, j)),
        out_shape=jax.ShapeDtypeStruct((depth, nb, 6 * D), F32),
        compiler_params=_cparams("parallel", "parallel"),
        name="adaln_mod",
    )(c_all, w_ada, b_ada.reshape(depth, 1, 6 * D))


def _mod_spec(mode, tm, k, tiles_per_batch):
    if mode == "row":
        return pl.BlockSpec((1, 1, D), lambda i, *_: (i // tiles_per_batch, 0, k))
    return pl.BlockSpec((tm, D), lambda i, *_: (i, k))


def _mod_val(ref, mode):
    return ref[0] if mode == "row" else ref[...]


def _ln0_kernel(x_ref, g_ref, b_ref, o_ref):
    o_ref[...] = _layer_norm(x_ref[...], g_ref[...], b_ref[...])


def _ln0(x, g, b, tm):
    n = x.shape[0]
    return pl.pallas_call(
        _ln0_kernel,
        grid=(n // tm,),
        in_specs=[pl.BlockSpec((tm, D), lambda i: (i, 0)),
                  pl.BlockSpec((1, D), lambda i: (0, 0)),
                  pl.BlockSpec((1, D), lambda i: (0, 0))],
        out_specs=pl.BlockSpec((tm, D), lambda i: (i, 0)),
        out_shape=jax.ShapeDtypeStruct((n, D), F32),
        compiler_params=_cparams("parallel"),
        name="ln0",
    )(x, g.reshape(1, D), b.reshape(1, D))


def _inproj_kernel(x_ref, sc_ref, sh_ref, w_ref, o_ref, h_scr, *, mode):
    @pl.when(pl.program_id(1) == 0)
    def _():
        h = x_ref[...] * (1.0 + _mod_val(sc_ref, mode)) + _mod_val(sh_ref, mode)
        h_scr[...] = h.astype(BF16)

    o_ref[...] = jnp.dot(h_scr[...], w_ref[...], preferred_element_type=F32)


def _inproj(x, mods, w_in_p, mode, tm, tiles_per_batch):
    n = x.shape[0]
    tn = 1152
    return pl.pallas_call(
        functools.partial(_inproj_kernel, mode=mode),
        grid=(n // tm, NPROJ // tn),
        in_specs=[
            pl.BlockSpec((tm, D), lambda i, j: (i, 0)),
            _mod_spec(mode, tm, 1, tiles_per_batch),
            _mod_spec(mode, tm, 0, tiles_per_batch),
            pl.BlockSpec((D, tn), lambda i, j: (0, j)),
        ],
        out_specs=pl.BlockSpec((tm, tn), lambda i, j: (i, j)),
        out_shape=jax.ShapeDtypeStruct((n, NPROJ), F32),
        scratch_shapes=[pltpu.VMEM((tm, D), BF16)],
        compiler_params=_cparams("parallel", "arbitrary"),
        name="inproj",
    )(x, mods, mods, w_in_p)


def _mixa_kernel(qkv_ref, ga_ref, sm_ref, cst_ref, s0_ref, cw_ref, alog_ref, dtb_ref, na_ref,
                 oa_ref, cso_ref, so_ref, xbuf, s_scr, *, C, tv, nc):
    c = pl.program_id(1)
    row = lax.broadcasted_iota(I32, (C, 1), 0)
    ri = lax.broadcasted_iota(I32, (C, C), 0)
    ci = lax.broadcasted_iota(I32, (C, C), 1)
    causal = ri >= ci
    strict = ri > ci
    eye = (ri == ci).astype(F32)
    tril = causal.astype(F32)

    @pl.when(c == 0)
    def _():
        xbuf[0:SUBLANES, :] = jnp.zeros((SUBLANES, CONV_DIM), F32)
        xbuf[SUBLANES - (CONV_W - 1):SUBLANES, :] = cst_ref[0]
        s_scr[...] = s0_ref[0]

    xbuf[SUBLANES:SUBLANES + C, :] = qkv_ref[...]

    sm = sm_ref[...]
    g_all = -jnp.exp(alog_ref[...]) * _softplus(sm + dtb_ref[...])
    beta_all = _sigmoid(sm)
    if tv < C:
        g_all = jnp.where(row < tv, g_all, 0.0)
        beta_all = jnp.where(row < tv, beta_all, 0.0)
    gcum_all = _dot(tril, g_all, hi=True)
    gcum_t = gcum_all.T
    eg_all = jnp.exp(gcum_all)
    n_nil = min(C, tv)
    n_double = max(int(math.ceil(math.log2(n_nil))) - 1, 0)

    for h in range(NH):
        cs = slice(h * HD, (h + 1) * HD)
        conv = []
        for part in range(3):
            col = slice(part * D + h * HD, part * D + (h + 1) * HD)
            acc = None
            for j in range(CONV_W):
                lo = SUBLANES - (CONV_W - 1) + j
                term = xbuf[lo:lo + C, col] * cw_ref[j:j + 1, col]
                acc = term if acc is None else acc + term
            conv.append(_silu(acc))
        q, k, v = conv
        q = q * lax.rsqrt(jnp.sum(q * q, -1, keepdims=True) + RMS_EPS) * (HD ** -0.5)
        k = k * lax.rsqrt(jnp.sum(k * k, -1, keepdims=True) + RMS_EPS)

        gc = gcum_all[:, h:h + 1]
        gr = gcum_t[h:h + 1, :]
        bt = beta_all[:, NH + h:NH + h + 1]
        eg = eg_all[:, h:h + 1]
        glast = gcum_all[tv - 1:tv, h:h + 1]

        dec = jnp.where(causal, jnp.exp(jnp.where(causal, gc - gr, 0.0)), 0.0)
        kb = k * bt
        a = -jnp.where(strict, _dot_nt(kb, k, hi=True) * dec, 0.0)
        p = eye + a
        qpow = a
        for _ in range(n_double):
            qpow = _dot(qpow, qpow, hi=True)
            p = p + _dot(p, qpow, hi=True)
        rhs = jnp.concatenate([kb * eg, v * bt], axis=1)
        sol = _dot(p, rhs, hi=True)
        w = sol[:, :HD]
        u = sol[:, HD:]
        s = s_scr[h]
        vn = u - _dot(w, s, hi=True)
        qk = jnp.where(causal, _dot_nt(q, k, hi=True) * dec, 0.0)
        o = _dot(q * eg, s, hi=True) + _dot(qk, vn, hi=True)
        kg = k * jnp.exp(glast - gc)
        s_scr[h] = jnp.exp(glast) * s + _dot_tn(kg, vn, hi=True)

        y = o * lax.rsqrt(jnp.mean(o * o, -1, keepdims=True) + RMS_EPS) * na_ref[...]
        oa_ref[:, cs] = y * _silu(ga_ref[:, cs])

    if nc > 1:
        xbuf[0:SUBLANES, :] = xbuf[C:C + SUBLANES, :]

    @pl.when(c == nc - 1)
    def _():
        if nc > 1:
            cso_ref[0] = xbuf[SUBLANES - (CONV_W - 1):SUBLANES, :]
        else:
            cso_ref[0] = xbuf[SUBLANES + tv - (CONV_W - 1):SUBLANES + tv, :]
        so_ref[0] = s_scr[...]


def _mixer_a(proj, conv_state, s0, conv_w, a_log_p, dt_bias_p, norm_a, nb, C, tv):
    n = proj.shape[0]
    nc = n // (nb * C)
    assert nc == 1 or tv == C
    return pl.pallas_call(
        functools.partial(_mixa_kernel, C=C, tv=tv, nc=nc),
        grid=(nb, nc),
        in_specs=[
            pl.BlockSpec((C, CONV_DIM), lambda b, c: (b * nc + c, C_QKV // CONV_DIM)),
            pl.BlockSpec((C, D), lambda b, c: (b * nc + c, C_GA // D)),
            pl.BlockSpec((C, LANES), lambda b, c: (b * nc + c, C_SMALL // LANES)),
            pl.BlockSpec((1, CONV_W - 1, CONV_DIM), lambda b, c: (b, 0, 0)),
            pl.BlockSpec((1, NH, HD, HD), lambda b, c: (b, 0, 0, 0)),
            pl.BlockSpec((CONV_W, CONV_DIM), lambda b, c: (0, 0)),
            pl.BlockSpec((1, LANES), lambda b, c: (0, 0)),
            pl.BlockSpec((1, LANES), lambda b, c: (0, 0)),
            pl.BlockSpec((1, HD), lambda b, c: (0, 0)),
        ],
        out_specs=[
            pl.BlockSpec((C, D), lambda b, c: (b * nc + c, 0)),
            pl.BlockSpec((1, CONV_W - 1, CONV_DIM), lambda b, c: (b, 0, 0)),
            pl.BlockSpec((1, NH, HD, HD), lambda b, c: (b, 0, 0, 0)),
        ],
        out_shape=[
            jax.ShapeDtypeStruct((n, D), F32),
            jax.ShapeDtypeStruct((nb, CONV_W - 1, CONV_DIM), F32),
            jax.ShapeDtypeStruct((nb, NH, HD, HD), F32),
        ],
        scratch_shapes=[pltpu.VMEM((C + 2 * SUBLANES, CONV_DIM), F32),
                        pltpu.VMEM((NH, HD, HD), F32)],
        compiler_params=_cparams("parallel", "arbitrary"),
        name="mixer_a",
    )(proj, proj, proj, conv_state, s0, conv_w, a_log_p, dt_bias_p, norm_a)


DIAG = SUBLANES


def _mixb_kernel(qb_ref, fb_ref, ib_ref, gb_ref, s0_ref, lbp_ref, nb_ref,
                 ob_ref, so_ref, st_scr, *, C, tv, nc, layer):
    c = pl.program_id(1)
    row = lax.broadcasted_iota(I32, (C, 1), 0)
    ri = lax.broadcasted_iota(I32, (C, C), 0)
    ci = lax.broadcasted_iota(I32, (C, C), 1)
    tril = (ri >= ci).astype(F32)
    row8 = lax.broadcasted_iota(I32, (DIAG, 1), 0)

    @pl.when(c == 0)
    def _():
        for h in range(NH):
            st_scr[h] = s0_ref[0, h].T

    lbp = lbp_ref[...]
    e = jnp.exp(lbp - jnp.max(lbp, axis=0, keepdims=True))
    psm = e / jnp.sum(e, axis=0, keepdims=True)
    lb = jnp.zeros((1, D), F32)
    for j in range(1, layer + 1):
        lb = lb + psm[j:j + 1, :]

    z = fb_ref[...]
    f = lb + (1.0 - lb) * _sigmoid(z)
    lf_all = jnp.log(jnp.maximum(f, F_MIN))
    kk_all = (1.0 - lb) * _sigmoid(-z)
    if tv < C:
        lf_all = jnp.where(row < tv, lf_all, 0.0)
        kk_all = jnp.where(row < tv, kk_all, 0.0)
    b_all = _dot(tril, lf_all, hi=True)

    levels = []
    lv = DIAG
    while lv < C:
        levels.append(lv)
        lv *= 2

    for h in range(NH):
        cs = slice(h * HD, (h + 1) * HD)
        q = _silu(qb_ref[:, cs])
        kk = kk_all[:, cs]
        iv = ib_ref[:, cs]
        b = b_all[:, cs]
        blast = b[tv - 1:tv, :]

        amat = None
        for lv in levels:
            pieces = []
            for j in range(C // lv):
                blk = b[j * lv:(j + 1) * lv, :]
                if j % 2 == 1:
                    pieces.append(blk - b[j * lv:j * lv + 1, :])
                else:
                    pieces.append(b[(j + 1) * lv:(j + 1) * lv + 1, :] - blk)
            ex = jnp.exp(jnp.concatenate(pieces, axis=0))
            lower = ((row // lv) % 2) == 1
            qs = jnp.where(lower, q * ex, 0.0)
            ks = jnp.where(lower, 0.0, kk * ex)
            blk_a = jnp.where((ri // (2 * lv)) == (ci // (2 * lv)), _dot_nt(qs, ks, hi=True), 0.0)
            amat = blk_a if amat is None else amat + blk_a

        od = []
        for j in range(C // DIAG):
            rs = slice(j * DIAG, (j + 1) * DIAG)
            bb, qq, k8, i8 = b[rs, :], q[rs, :], kk[rs, :], iv[rs, :]
            acc = jnp.zeros((DIAG, HD), F32)
            for s in range(DIAG):
                ok = row8 >= s
                ex = jnp.exp(jnp.where(ok, bb - bb[s:s + 1, :], 0.0))
                zz = jnp.where(ok, qq * ex * k8[s:s + 1, :], 0.0)
                acc = acc + jnp.sum(zz, -1, keepdims=True) * i8[s:s + 1, :]
            od.append(acc)
        o = od[0] if len(od) == 1 else jnp.concatenate(od, axis=0)
        if amat is not None:
            o = o + _dot(amat, iv, hi=True)

        st = st_scr[h]
        o = o + _dot_nt(q * jnp.exp(b), st, hi=True)
        kg = kk * jnp.exp(blast - b)
        st_scr[h] = st * jnp.exp(blast) + _dot_tn(iv, kg, hi=True)

        y = o * lax.rsqrt(jnp.mean(o * o, -1, keepdims=True) + RMS_EPS) * nb_ref[...]
        ob_ref[:, cs] = y * _silu(gb_ref[:, cs])

    @pl.when(c == nc - 1)
    def _():
        for h in range(NH):
            so_ref[0, h] = st_scr[h].T


def _mixer_b(proj, s0, lb_param, norm_b, nb, C, tv, layer):
    n = proj.shape[0]
    nc = n // (nb * C)
    depth = lb_param.shape[0]
    return pl.pallas_call(
        functools.partial(_mixb_kernel, C=C, tv=tv, nc=nc, layer=layer),
        grid=(nb, nc),
        in_specs=[
            pl.BlockSpec((C, D), lambda b, c: (b * nc + c, C_QB // D)),
            pl.BlockSpec((C, D), lambda b, c: (b * nc + c, C_FB // D)),
            pl.BlockSpec((C, D), lambda b, c: (b * nc + c, C_IB // D)),
            pl.BlockSpec((C, D), lambda b, c: (b * nc + c, C_GB // D)),
            pl.BlockSpec((1, NH, HD, HD), lambda b, c: (b, 0, 0, 0)),
            pl.BlockSpec((depth, D), lambda b, c: (0, 0)),
            pl.BlockSpec((1, HD), lambda b, c: (0, 0)),
        ],
        out_specs=[
            pl.BlockSpec((C, D), lambda b, c: (b * nc + c, 0)),
            pl.BlockSpec((1, NH, HD, HD), lambda b, c: (b, 0, 0, 0)),
        ],
        out_shape=[
            jax.ShapeDtypeStruct((n, D), F32),
            jax.ShapeDtypeStruct((nb, NH, HD, HD), F32),
        ],
        scratch_shapes=[pltpu.VMEM((NH, HD, HD), F32)],
        compiler_params=_cparams("parallel", "arbitrary"),
        name="mixer_b",
    )(proj, proj, proj, proj, s0, lb_param, norm_b)


def _post_kernel(oa_ref, ob_ref, gt_ref, x_ref, g1_ref, sc2_ref, sh2_ref, wa_ref, wb_ref, wo_ref,
                 lg_ref, lb_ref, wr_ref, br_ref, x1_ref, h2_ref, rw_ref, re_ref, *, mode, alpha):
    gates = _sigmoid(gt_ref[...])
    merged = gates[:, :D] * _dot(oa_ref[...], wa_ref[...]) + gates[:, D:] * _dot(ob_ref[...], wb_ref[...])
    y = _dot(merged, wo_ref[...])
    x1 = _layer_norm(alpha * x_ref[...] + (1.0 + _mod_val(g1_ref, mode)) * y, lg_ref[...], lb_ref[...])
    x1_ref[...] = x1
    h2 = x1 * (1.0 + _mod_val(sc2_ref, mode)) + _mod_val(sh2_ref, mode)
    h2_ref[...] = h2

    logits = _dot(h2, wr_ref[...], hi=True) + br_ref[...]
    lane = lax.broadcasted_iota(I32, logits.shape, 1)
    neg = jnp.float32(-jnp.inf)
    gl = jnp.where(lane < N_GROUPS, logits, neg)
    gmax = jnp.max(gl, -1, keepdims=True)
    grp = jnp.min(jnp.where(gl == gmax, lane, LANES), -1, keepdims=True)
    p_grp = 1.0 / jnp.sum(jnp.exp(gl - gmax), -1, keepdims=True)
    lo = N_GROUPS + grp * EPG
    el = jnp.where((lane >= lo) & (lane < lo + EPG), logits, neg)
    v1 = jnp.max(el, -1, keepdims=True)
    i1 = jnp.min(jnp.where(el == v1, lane, LANES), -1, keepdims=True)
    el2 = jnp.where(lane == i1, neg, el)
    v2 = jnp.max(el2, -1, keepdims=True)
    i2 = jnp.min(jnp.where(el2 == v2, lane, LANES), -1, keepdims=True)
    e2 = jnp.exp(v2 - v1)
    w1 = p_grp / (1.0 + e2)
    w2 = p_grp * e2 / (1.0 + e2)
    rw_ref[...] = jnp.where(lane == 0, w1, jnp.where(lane == 1, w2, 0.0))
    re_ref[...] = jnp.where(lane == 0, i1 - N_GROUPS, jnp.where(lane == 1, i2 - N_GROUPS, 0))


def _post_mixer(o_a, o_b, proj, x, mods, w_br_a, w_br_b, w_o, ln_g, ln_b, w_r, b_r, mode, tm,
                tiles_per_batch, alpha):
    n = x.shape[0]
    full = lambda shape: pl.BlockSpec(shape, lambda i: tuple(0 for _ in shape))
    tok = pl.BlockSpec((tm, D), lambda i: (i, 0))
    lane_out = pl.BlockSpec((tm, LANES), lambda i: (i, 0))
    return pl.pallas_call(
        functools.partial(_post_kernel, mode=mode, alpha=alpha),
        grid=(n // tm,),
        in_specs=[
            tok, tok,
            pl.BlockSpec((tm, 2 * D), lambda i: (i, C_MERGE // (2 * D))),
            tok,
            _mod_spec(mode, tm, 2, tiles_per_batch),
            _mod_spec(mode, tm, 4, tiles_per_batch),
            _mod_spec(mode, tm, 3, tiles_per_batch),
            full((D, D)), full((D, D)), full((D, D)),
            full((1, D)), full((1, D)),
            full((D, LANES)), full((1, LANES)),
        ],
        out_specs=[tok, tok, lane_out, lane_out],
        out_shape=[
            jax.ShapeDtypeStruct((n, D), F32),
            jax.ShapeDtypeStruct((n, D), F32),
            jax.ShapeDtypeStruct((n, LANES), F32),
            jax.ShapeDtypeStruct((n, LANES), I32),
        ],
        compiler_params=_cparams("parallel"),
        name="post_mixer",
    )(o_a, o_b, proj, x, mods, mods, mods, w_br_a, w_br_b, w_o, ln_g, ln_b, w_r, b_r)


TM_EXP = 256


def _expert_kernel(be_ref, nblk_ref, tok_ref, h_hbm, wg_ref, wu_ref, wd_ref, ys_ref, xs, sem):
    i = pl.program_id(0)

    @pl.when(i < nblk_ref[0])
    def _():
        def row_copy(r):
            return pltpu.make_async_copy(h_hbm.at[pl.ds(tok_ref[r], 1), :], xs.at[pl.ds(r, 1), :], sem)

        def start(r, carry):
            row_copy(r).start()
            return carry

        def wait(r, carry):
            row_copy(r).wait()
            return carry

        lax.fori_loop(0, TM_EXP, start, 0)
        lax.fori_loop(0, TM_EXP, wait, 0)
        x = xs[...]
        hid = _silu(_dot(x, wg_ref[0])) * _dot(x, wu_ref[0])
        ys_ref[...] = _dot(hid, wd_ref[0])

    @pl.when(i >= nblk_ref[0])
    def _():
        ys_ref[...] = jnp.zeros(ys_ref.shape, F32)


def _experts(h2, slot_tok, block_e, n_used, w_gate, w_up, w_down):
    n_rows = slot_tok.shape[0]
    n_blocks = n_rows // TM_EXP
    grid_spec = pltpu.PrefetchScalarGridSpec(
        num_scalar_prefetch=2,
        grid=(n_blocks,),
        in_specs=[
            pl.BlockSpec((TM_EXP,), lambda i, be, nb: (i,), memory_space=pltpu.SMEM),
            pl.BlockSpec(memory_space=pl.ANY),
            pl.BlockSpec((1, D, D_EXP), lambda i, be, nb: (be[i], 0, 0)),
            pl.BlockSpec((1, D, D_EXP), lambda i, be, nb: (be[i], 0, 0)),
            pl.BlockSpec((1, D_EXP, D), lambda i, be, nb: (be[i], 0, 0)),
        ],
        out_specs=pl.BlockSpec((TM_EXP, D), lambda i, be, nb: (i, 0)),
        scratch_shapes=[pltpu.VMEM((TM_EXP, D), F32), pltpu.SemaphoreType.DMA],
    )
    return pl.pallas_call(
        _expert_kernel,
        grid_spec=grid_spec,
        out_shape=jax.ShapeDtypeStruct((n_rows, D), F32),
        compiler_params=_cparams("arbitrary"),
        name="experts",
    )(block_e, n_used, slot_tok, h2, w_gate, w_up, w_down)


def _combine_kernel(d0_ref, d1_ref, ys_hbm, rw_ref, x_ref, g2_ref, lg_ref, lb_ref, o_ref,
                    y0, y1, sem, *, mode, alpha, tm):
    def copies(r):
        return (pltpu.make_async_copy(ys_hbm.at[pl.ds(d0_ref[r], 1), :], y0.at[pl.ds(r, 1), :], sem),
                pltpu.make_async_copy(ys_hbm.at[pl.ds(d1_ref[r], 1), :], y1.at[pl.ds(r, 1), :], sem))

    def start(r, carry):
        a, b = copies(r)
        a.start()
        b.start()
        return carry

    def wait(r, carry):
        a, b = copies(r)
        a.wait()
        b.wait()
        return carry

    lax.fori_loop(0, tm, start, 0)
    lax.fori_loop(0, tm, wait, 0)
    rw = rw_ref[...]
    y = rw[:, 0:1] * y0[...] + rw[:, 1:2] * y1[...]
    o_ref[...] = _layer_norm(alpha * x_ref[...] + (1.0 + _mod_val(g2_ref, mode)) * y,
                             lg_ref[...], lb_ref[...])


def _combine(ys, d0, d1, rw, x1, mods, ln_g, ln_b, mode, tm, tiles_per_batch, alpha):
    n = x1.shape[0]
    return pl.pallas_call(
        functools.partial(_combine_kernel, mode=mode, alpha=alpha, tm=tm),
        grid=(n // tm,),
        in_specs=[
            pl.BlockSpec((tm,), lambda i: (i,), memory_space=pltpu.SMEM),
            pl.BlockSpec((tm,), lambda i: (i,), memory_space=pltpu.SMEM),
            pl.BlockSpec(memory_space=pl.ANY),
            pl.BlockSpec((tm, LANES), lambda i: (i, 0)),
            pl.BlockSpec((tm, D), lambda i: (i, 0)),
            _mod_spec(mode, tm, 5, tiles_per_batch),
            pl.BlockSpec((1, D), lambda i: (0, 0)),
            pl.BlockSpec((1, D), lambda i: (0, 0)),
        ],
        out_specs=pl.BlockSpec((tm, D), lambda i: (i, 0)),
        out_shape=jax.ShapeDtypeStruct((n, D), F32),
        scratch_shapes=[pltpu.VMEM((tm, D), F32), pltpu.VMEM((tm, D), F32), pltpu.SemaphoreType.DMA],
        compiler_params=_cparams("arbitrary"),
        name="combine",
    )(d0, d1, ys, rw, x1, mods, ln_g, ln_b)


def _route_meta(eid):
    n = eid.shape[0]
    flat = eid.reshape(-1)
    n_slots = flat.shape[0]
    onehot = (flat[:, None] == jnp.arange(N_EXP, dtype=I32)[None, :]).astype(I32)
    csum = jnp.cumsum(onehot, axis=0)
    counts = csum[-1]
    rank = jnp.take_along_axis(csum - onehot, flat[:, None], axis=1)[:, 0]
    padded = (counts + TM_EXP - 1) // TM_EXP * TM_EXP
    pad_end = jnp.cumsum(padded)
    pad_start = pad_end - padded
    dest = pad_start[flat] + rank
    n_blocks = (n_slots + N_EXP * (TM_EXP - 1)) // TM_EXP
    tok = jnp.repeat(jnp.arange(n, dtype=I32), 2)
    slot_tok = jnp.zeros((n_blocks * TM_EXP,), I32).at[dest].set(tok)
    block_start = jnp.arange(n_blocks, dtype=I32) * TM_EXP
    block_e = jnp.minimum(jnp.sum(pad_end[None, :] <= block_start[:, None], axis=1), N_EXP - 1).astype(I32)
    n_used = (pad_end[-1:] // TM_EXP).astype(I32)
    dest = dest.reshape(n, 2)
    return slot_tok, block_e, n_used, dest[:, 0], dest[:, 1]


def _pack_w_in(w_in):
    off_g_a = CONV_DIM
    off_decay = off_g_a + D
    off_q_b = off_decay + 2 * NH
    off_merge = off_q_b + 4 * D
    depth = w_in.shape[0]
    small = w_in[:, :, off_decay:off_q_b]
    pad = jnp.zeros((depth, D, LANES - 2 * NH), w_in.dtype)
    packed = jnp.concatenate([w_in[:, :, :off_decay], w_in[:, :, off_q_b:off_merge + 2 * D], small, pad], axis=2)
    return packed.astype(BF16)


def _pad_lanes(v):
    depth, k = v.shape
    return jnp.concatenate([v, jnp.zeros((depth, LANES - k), v.dtype)], axis=1).reshape(depth, 1, LANES)


def kernel(x_prompt, x_sample, state_conv, state_delta, state_hgrn, c_prompt, c_sample, ln0_g, ln0_b, w_ada, b_ada, w_in, conv_w, a_log, dt_bias, norm_a, lb_param, norm_b, w_br_a, w_br_b, w_o, ln1_g, ln1_b, w_grp, b_grp, w_rt, b_rt, w_gate_e, w_up_e, w_down_e, ln2_g, ln2_b):
    depth = w_in.shape[0]
    alpha = (2 * depth) ** 0.25
    bp, tp, _ = x_prompt.shape
    bs, ts, _ = x_sample.shape
    n_p = bp * tp
    tsp = SUBLANES
    assert ts <= tsp
    n_s = bs * tsp

    w_in_p = _pack_w_in(w_in)
    w_a16, w_b16, w_o16 = w_br_a.astype(BF16), w_br_b.astype(BF16), w_o.astype(BF16)
    w_r = jnp.concatenate([w_grp, w_rt, jnp.zeros((depth, D, LANES - N_GROUPS - N_EXP), F32)], axis=2)
    b_r = jnp.concatenate([b_grp, b_rt, jnp.zeros((depth, LANES - N_GROUPS - N_EXP), F32)], axis=1)
    b_r = b_r.reshape(depth, 1, LANES)
    a_log_p, dt_bias_p = _pad_lanes(a_log), _pad_lanes(dt_bias)

    mods = _modulation(jnp.concatenate([c_prompt, c_sample], axis=0), w_ada, b_ada)
    mods_p = mods[:, :bp].reshape(depth, bp, 1, 6 * D)
    mods_s = jnp.repeat(mods[:, bp:], tsp, axis=1)

    xs_pad = jnp.concatenate([x_sample, jnp.zeros((bs, tsp - ts, D), x_sample.dtype)], axis=1)
    cp = math.gcd(tp, 64)
    tm_p = math.gcd(tp, 512)
    tm_s = math.gcd(n_s, 512)
    groups = [
        dict(mode="row", n=n_p, nb=bp, C=cp, tv=cp, tm=tm_p, tpb=tp // tm_p,
             conv=jnp.zeros((depth, bp, CONV_W - 1, CONV_DIM), F32),
             delta=jnp.zeros((depth, bp, NH, HD, HD), F32),
             hgrn=jnp.zeros((depth, bp, NH, HD, HD), F32)),
        dict(mode="tok", n=n_s, nb=bs, C=tsp, tv=ts, tm=tm_s, tpb=1,
             conv=state_conv, delta=state_delta, hgrn=state_hgrn),
    ]
    xs = [_ln0(x_prompt.reshape(n_p, D), ln0_g, ln0_b, tm_p),
          _ln0(xs_pad.reshape(n_s, D), ln0_g, ln0_b, tm_s)]
    outs = [dict(conv=[], delta=[], hgrn=[]) for _ in groups]

    for l in range(depth):
        x1s, h2s, rws, res = [], [], [], []
        for gi, g in enumerate(groups):
            m = mods_p[l] if g["mode"] == "row" else mods_s[l]
            proj = _inproj(xs[gi], m, w_in_p[l], g["mode"], g["tm"], g["tpb"])
            o_a, cso, sdo = _mixer_a(proj, g["conv"][l], g["delta"][l], conv_w[l], a_log_p[l], dt_bias_p[l],
                                     norm_a[l].reshape(1, HD), g["nb"], g["C"], g["tv"])
            o_b, sho = _mixer_b(proj, g["hgrn"][l], lb_param, norm_b[l].reshape(1, HD), g["nb"], g["C"],
                                g["tv"], l)
            x1, h2, rw, re = _post_mixer(o_a, o_b, proj, xs[gi], m, w_a16[l], w_b16[l], w_o16[l],
                                         ln1_g[l].reshape(1, D), ln1_b[l].reshape(1, D), w_r[l], b_r[l],
                                         g["mode"], g["tm"], g["tpb"], alpha)
            outs[gi]["conv"].append(cso)
            outs[gi]["delta"].append(sdo)
            outs[gi]["hgrn"].append(sho)
            x1s.append(x1)
            h2s.append(h2)
            rws.append(rw)
            res.append(re)
        h2_all = jnp.concatenate(h2s, axis=0)
        eid = jnp.concatenate(res, axis=0)[:, :2]
        slot_tok, block_e, n_used, d0, d1 = _route_meta(eid)
        ys = _experts(h2_all, slot_tok, block_e, n_used, w_gate_e[l], w_up_e[l], w_down_e[l])
        off = 0
        for gi, g in enumerate(groups):
            m = mods_p[l] if g["mode"] == "row" else mods_s[l]
            sl = slice(off, off + g["n"])
            xs[gi] = _combine(ys, d0[sl], d1[sl], rws[gi], x1s[gi], m, ln2_g[l].reshape(1, D),
                              ln2_b[l].reshape(1, D), g["mode"], g["tm"], g["tpb"], alpha)
            off += g["n"]

    y_prompt = xs[0].reshape(bp, tp, D)
    y_sample = xs[1].reshape(bs, tsp, D)[:, :ts]
    return (y_prompt, y_sample,
            jnp.stack(outs[0]["conv"]), jnp.stack(outs[0]["delta"]), jnp.stack(outs[0]["hgrn"]),
            jnp.stack(outs[1]["conv"]), jnp.stack(outs[1]["delta"]), jnp.stack(outs[1]["hgrn"]))
```

```python
import functools
import math

import jax
import jax.numpy as jnp
from jax import lax
from jax.experimental import pallas as pl
from jax.experimental.pallas import tpu as pltpu

F32 = jnp.float32
BF16 = jnp.bfloat16
I32 = jnp.int32

D = 1024
NH = 8
HD = 128
CONV_W = 4
CONV_DIM = 3 * D
N_GROUPS = 4
EPG = 8
N_EXP = N_GROUPS * EPG
D_EXP = 512
F_MIN = 1e-30
LN_EPS = 1e-5
RMS_EPS = 1e-6

C_QKV = 0
C_GA = 3 * D
C_QB = 4 * D
C_FB = 5 * D
C_IB = 6 * D
C_GB = 7 * D
C_MERGE = 8 * D
C_SMALL = 10 * D
NPROJ = 10 * D + 128

LANES = 128
SUBLANES = 8
VMEM_LIMIT = 48 * 1024 * 1024

HI = lax.Precision.HIGHEST


def _cparams(*sem):
    return pltpu.CompilerParams(dimension_semantics=sem, vmem_limit_bytes=VMEM_LIMIT)


def _dot(a, b, hi=False):
    if hi:
        return jnp.dot(a, b, preferred_element_type=F32, precision=HI)
    return jnp.dot(a.astype(BF16), b.astype(BF16), preferred_element_type=F32)


def _dot_nt(a, b):
    return lax.dot_general(a.astype(BF16), b.astype(BF16), (((1,), (1,)), ((), ())),
                           preferred_element_type=F32)


def _dot_tn(a, b):
    return lax.dot_general(a.astype(BF16), b.astype(BF16), (((0,), (0,)), ((), ())),
                           preferred_element_type=F32)


def _split(a):
    hi = a.astype(BF16)
    return hi, (a - hi.astype(F32)).astype(BF16)


def _dot3(a, b):
    mm = lambda x, y: jnp.dot(x, y, preferred_element_type=F32)
    return mm(a[0], b[0]) + (mm(a[0], b[1]) + mm(a[1], b[0]))


def _sigmoid(x):
    return 1.0 / (1.0 + jnp.exp(-x))


def _silu(x):
    return x * _sigmoid(x)


def _softplus(x):
    return jnp.maximum(x, 0.0) + jnp.log(1.0 + jnp.exp(-jnp.abs(x)))


def _layer_norm(x, g, b):
    mu = jnp.mean(x, -1, keepdims=True)
    xc = x - mu
    var = jnp.mean(xc * xc, -1, keepdims=True)
    return xc * lax.rsqrt(var + LN_EPS) * g + b


def _layer_spec(shape, l):
    zeros = (0,) * len(shape)
    return pl.BlockSpec((1,) + tuple(shape), lambda *_: (l,) + zeros)


def _mod_kernel(c_ref, w_ref, b_ref, o_ref):
    ca = _silu(c_ref[...])
    o_ref[0] = _dot(ca, w_ref[0]) + b_ref[0]


def _modulation(c_all, w_ada, b_ada):
    depth = w_ada.shape[0]
    nb = c_all.shape[0]
    tn = 1536
    return pl.pallas_call(
        _mod_kernel,
        grid=(depth, 6 * D // tn),
        in_specs=[
            pl.BlockSpec((nb, D), lambda l, j: (0, 0)),
            pl.BlockSpec((1, D, tn), lambda l, j: (l, 0, j)),
            pl.BlockSpec((1, 1, tn), lambda l, j: (l, 0, j)),
        ],
        out_specs=pl.BlockSpec((1, nb, tn), lambda l, j: (l, 0, j)),
        out_shape=jax.ShapeDtypeStruct((depth, nb, 6 * D), F32),
        compiler_params=_cparams("parallel", "parallel"),
        name="adaln_mod",
    )(c_all, w_ada, b_ada.reshape(depth, 1, 6 * D))


def _mod_spec(mode, l, tm, k, tiles_per_batch):
    if mode == "row":
        return pl.BlockSpec((1, 1, 1, D), lambda i, *_: (l, i // tiles_per_batch, 0, k))
    return pl.BlockSpec((1, tm, D), lambda i, *_: (l, i, k))


def _mod_val(ref, mode):
    return ref[0, 0] if mode == "row" else ref[0]


def _ln0_kernel(x_ref, g_ref, b_ref, o_ref):
    o_ref[...] = _layer_norm(x_ref[...], g_ref[...], b_ref[...])


def _ln0(x, g, b, tm):
    n = x.shape[0]
    return pl.pallas_call(
        _ln0_kernel,
        grid=(n // tm,),
        in_specs=[pl.BlockSpec((tm, D), lambda i: (i, 0)),
                  pl.BlockSpec((1, D), lambda i: (0, 0)),
                  pl.BlockSpec((1, D), lambda i: (0, 0))],
        out_specs=pl.BlockSpec((tm, D), lambda i: (i, 0)),
        out_shape=jax.ShapeDtypeStruct((n, D), F32),
        compiler_params=_cparams("parallel"),
        name="ln0",
    )(x, g.reshape(1, D), b.reshape(1, D))


def _inproj_kernel(x_ref, sc_ref, sh_ref, w_ref, o_ref, h_scr, *, mode):
    @pl.when(pl.program_id(1) == 0)
    def _():
        h = x_ref[...] * (1.0 + _mod_val(sc_ref, mode)) + _mod_val(sh_ref, mode)
        h_scr[...] = h.astype(BF16)

    o_ref[...] = jnp.dot(h_scr[...], w_ref[0], preferred_element_type=F32)


def _inproj(x, mods, w_in_p, l, mode, tm, tiles_per_batch):
    n = x.shape[0]
    tn = 1152
    return pl.pallas_call(
        functools.partial(_inproj_kernel, mode=mode),
        grid=(n // tm, NPROJ // tn),
        in_specs=[
            pl.BlockSpec((tm, D), lambda i, j: (i, 0)),
            _mod_spec(mode, l, tm, 1, tiles_per_batch),
            _mod_spec(mode, l, tm, 0, tiles_per_batch),
            pl.BlockSpec((1, D, tn), lambda i, j: (l, 0, j)),
        ],
        out_specs=pl.BlockSpec((tm, tn), lambda i, j: (i, j)),
        out_shape=jax.ShapeDtypeStruct((n, NPROJ), F32),
        scratch_shapes=[pltpu.VMEM((tm, D), BF16)],
        compiler_params=_cparams("parallel", "arbitrary"),
        name="inproj",
    )(x, mods, mods, w_in_p)


def _mixa_kernel(qkv_ref, ga_ref, sm_ref, cst_ref, s0_ref, cw_ref, alog_ref, dtb_ref, na_ref,
                 cbuf_ref, sbuf_ref, oa_ref, cso_ref, so_ref, xbuf, s_scr, *, C, tv, nc, bb):
    del cbuf_ref, sbuf_ref
    c = pl.program_id(1)
    row = lax.broadcasted_iota(I32, (C, 1), 0)
    ri = lax.broadcasted_iota(I32, (C, C), 0)
    ci = lax.broadcasted_iota(I32, (C, C), 1)
    causal = ri >= ci
    strict = ri > ci
    eye = (ri == ci).astype(F32)
    tril = causal.astype(F32)
    hist = SUBLANES - (CONV_W - 1)

    @pl.when(c == 0)
    def _():
        for bi in range(bb):
            xbuf[bi, 0:SUBLANES, :] = jnp.zeros((SUBLANES, CONV_DIM), F32)
            xbuf[bi, hist:SUBLANES, :] = cst_ref[0, bi]
            s_scr[bi] = s0_ref[0, bi]

    pre = []
    for bi in range(bb):
        xbuf[bi, SUBLANES:SUBLANES + C, :] = qkv_ref[bi]
        sm = sm_ref[bi]
        g_all = -jnp.exp(alog_ref[0]) * _softplus(sm + dtb_ref[0])
        beta_all = _sigmoid(sm)
        if tv < C:
            g_all = jnp.where(row < tv, g_all, 0.0)
            beta_all = jnp.where(row < tv, beta_all, 0.0)
        gcum_all = _dot(tril, g_all, hi=True)
        pre.append((gcum_all, gcum_all.T, jnp.exp(gcum_all), beta_all))

    n_nil = min(C, tv)
    n_double = max(int(math.ceil(math.log2(n_nil))) - 1, 0)
    units = [(bi, h) for bi in range(bb) for h in range(NH)]

    st = []
    for bi, h in units:
        gcum_all, gcum_t, eg_all, beta_all = pre[bi]
        conv = []
        for part in range(3):
            col = slice(part * D + h * HD, part * D + (h + 1) * HD)
            acc = None
            for j in range(CONV_W):
                term = xbuf[bi, hist + j:hist + j + C, col] * cw_ref[0, j:j + 1, col]
                acc = term if acc is None else acc + term
            conv.append(_silu(acc))
        q, k, v = conv
        q = q * lax.rsqrt(jnp.sum(q * q, -1, keepdims=True) + RMS_EPS) * (HD ** -0.5)
        k = k * lax.rsqrt(jnp.sum(k * k, -1, keepdims=True) + RMS_EPS)
        gc = gcum_all[:, h:h + 1]
        gr = gcum_t[h:h + 1, :]
        bt = beta_all[:, NH + h:NH + h + 1]
        eg = eg_all[:, h:h + 1]
        glast = gcum_all[tv - 1:tv, h:h + 1]
        dec = jnp.where(causal, jnp.exp(jnp.where(causal, gc - gr, 0.0)), 0.0)
        kb = k * bt
        st.append(dict(
            a=-jnp.where(strict, _dot_nt(kb, k) * dec, 0.0),
            rhs=jnp.concatenate([kb * eg, v * bt], axis=1),
            qk=jnp.where(causal, _dot_nt(q, k) * dec, 0.0),
            qe=q * eg,
            kg=k * jnp.exp(glast - gc),
            eglast=jnp.exp(glast),
        ))

    ps = [eye + u["a"] for u in st]
    qsp = [_split(u["a"]) for u in st]
    for _ in range(n_double):
        qs = [_dot3(s2, s2) for s2 in qsp]
        qsp = [_split(x) for x in qs]
        ps = [p + _dot3(_split(p), s2) for p, s2 in zip(ps, qsp)]
    sols = [_dot3(_split(p), _split(u["rhs"])) for p, u in zip(ps, st)]

    for (bi, h), u, sol in zip(units, st, sols):
        cs = slice(h * HD, (h + 1) * HD)
        s = s_scr[bi, h]
        ws = _dot(jnp.concatenate([sol[:, :HD], u["qe"]], axis=0), s)
        vn = sol[:, HD:] - ws[:C]
        o = ws[C:] + _dot(u["qk"], vn)
        s_scr[bi, h] = u["eglast"] * s + _dot_tn(u["kg"], vn)
        y = o * lax.rsqrt(jnp.mean(o * o, -1, keepdims=True) + RMS_EPS) * na_ref[0]
        oa_ref[bi, :, cs] = y * _silu(ga_ref[bi, :, cs])

    if nc > 1:
        for bi in range(bb):
            xbuf[bi, 0:SUBLANES, :] = xbuf[bi, C:C + SUBLANES, :]

    @pl.when(c == nc - 1)
    def _():
        for bi in range(bb):
            if nc > 1:
                cso_ref[0, bi] = xbuf[bi, hist:SUBLANES, :]
            else:
                cso_ref[0, bi] = xbuf[bi, SUBLANES + tv - (CONV_W - 1):SUBLANES + tv, :]
            so_ref[0, bi] = s_scr[bi]


def _mixer_a(proj3, conv_state, s0, conv_buf, s_buf, conv_w, a_log_p, dt_bias_p, norm_a, l, C, tv, bb):
    nb, t, _ = proj3.shape
    nc = t // C
    assert nc == 1 or tv == C
    return pl.pallas_call(
        functools.partial(_mixa_kernel, C=C, tv=tv, nc=nc, bb=bb),
        grid=(nb // bb, nc),
        in_specs=[
            pl.BlockSpec((bb, C, CONV_DIM), lambda b, c: (b, c, C_QKV // CONV_DIM)),
            pl.BlockSpec((bb, C, D), lambda b, c: (b, c, C_GA // D)),
            pl.BlockSpec((bb, C, LANES), lambda b, c: (b, c, C_SMALL // LANES)),
            pl.BlockSpec((1, bb, CONV_W - 1, CONV_DIM), lambda b, c: (l, b, 0, 0)),
            pl.BlockSpec((1, bb, NH, HD, HD), lambda b, c: (l, b, 0, 0, 0)),
            _layer_spec((CONV_W, CONV_DIM), l),
            _layer_spec((1, LANES), l),
            _layer_spec((1, LANES), l),
            _layer_spec((1, HD), l),
            pl.BlockSpec(memory_space=pl.ANY),
            pl.BlockSpec(memory_space=pl.ANY),
        ],
        out_specs=[
            pl.BlockSpec((bb, C, D), lambda b, c: (b, c, 0)),
            pl.BlockSpec((1, bb, CONV_W - 1, CONV_DIM), lambda b, c: (l, b, 0, 0)),
            pl.BlockSpec((1, bb, NH, HD, HD), lambda b, c: (l, b, 0, 0, 0)),
        ],
        out_shape=[
            jax.ShapeDtypeStruct((nb, t, D), F32),
            jax.ShapeDtypeStruct(conv_buf.shape, F32),
            jax.ShapeDtypeStruct(s_buf.shape, F32),
        ],
        input_output_aliases={9: 1, 10: 2},
        scratch_shapes=[pltpu.VMEM((bb, C + 2 * SUBLANES, CONV_DIM), F32),
                        pltpu.VMEM((bb, NH, HD, HD), F32)],
        compiler_params=_cparams("parallel", "arbitrary"),
        name="mixer_a",
    )(proj3, proj3, proj3, conv_state, s0, conv_w, a_log_p, dt_bias_p, norm_a, conv_buf, s_buf)


DIAG = SUBLANES


def _mixb_kernel(qb_ref, fb_ref, ib_ref, gb_ref, s0_ref, lbp_ref, nb_ref, sbuf_ref,
                 ob_ref, so_ref, st_scr, *, C, tv, nc, bb, layer):
    del sbuf_ref
    c = pl.program_id(1)
    row = lax.broadcasted_iota(I32, (C, 1), 0)
    ri = lax.broadcasted_iota(I32, (C, C), 0)
    ci = lax.broadcasted_iota(I32, (C, C), 1)
    tril = (ri >= ci).astype(F32)
    row8 = lax.broadcasted_iota(I32, (DIAG, 1), 0)

    @pl.when(c == 0)
    def _():
        for bi in range(bb):
            for h in range(NH):
                st_scr[bi, h] = s0_ref[0, bi, h].T

    lbp = lbp_ref[...]
    e = jnp.exp(lbp - jnp.max(lbp, axis=0, keepdims=True))
    psm = e / jnp.sum(e, axis=0, keepdims=True)
    lb = jnp.zeros((1, D), F32)
    for j in range(1, layer + 1):
        lb = lb + psm[j:j + 1, :]

    levels = []
    lv = DIAG
    while lv < C:
        levels.append(lv)
        lv *= 2

    for bi in range(bb):
        z = fb_ref[bi]
        f = lb + (1.0 - lb) * _sigmoid(z)
        lf_all = jnp.log(jnp.maximum(f, F_MIN))
        kk_all = (1.0 - lb) * _sigmoid(-z)
        if tv < C:
            lf_all = jnp.where(row < tv, lf_all, 0.0)
            kk_all = jnp.where(row < tv, kk_all, 0.0)
        b_all = _dot(tril, lf_all, hi=True)

        for h in range(NH):
            cs = slice(h * HD, (h + 1) * HD)
            q = _silu(qb_ref[bi, :, cs])
            kk = kk_all[:, cs]
            iv = ib_ref[bi, :, cs]
            b = b_all[:, cs]
            blast = b[tv - 1:tv, :]

            amat = None
            for lv in levels:
                pieces = []
                for j in range(C // lv):
                    blk = b[j * lv:(j + 1) * lv, :]
                    if j % 2 == 1:
                        pieces.append(blk - b[j * lv:j * lv + 1, :])
                    else:
                        pieces.append(b[(j + 1) * lv:(j + 1) * lv + 1, :] - blk)
                ex = jnp.exp(jnp.concatenate(pieces, axis=0))
                lower = ((row // lv) % 2) == 1
                qs = jnp.where(lower, q * ex, 0.0)
                ks = jnp.where(lower, 0.0, kk * ex)
                blk_a = jnp.where((ri // (2 * lv)) == (ci // (2 * lv)), _dot_nt(qs, ks), 0.0)
                amat = blk_a if amat is None else amat + blk_a

            od = []
            for j in range(C // DIAG):
                rs = slice(j * DIAG, (j + 1) * DIAG)
                bd, qq, k8, i8 = b[rs, :], q[rs, :], kk[rs, :], iv[rs, :]
                acc = jnp.zeros((DIAG, HD), F32)
                for s in range(DIAG):
                    ok = row8 >= s
                    ex = jnp.exp(jnp.where(ok, bd - bd[s:s + 1, :], 0.0))
                    zz = jnp.where(ok, qq * ex * k8[s:s + 1, :], 0.0)
                    acc = acc + jnp.sum(zz, -1, keepdims=True) * i8[s:s + 1, :]
                od.append(acc)
            o = od[0] if len(od) == 1 else jnp.concatenate(od, axis=0)
            if amat is not None:
                o = o + _dot(amat, iv)

            st = st_scr[bi, h]
            o = o + _dot_nt(q * jnp.exp(b), st)
            kg = kk * jnp.exp(blast - b)
            st_scr[bi, h] = st * jnp.exp(blast) + _dot_tn(iv, kg)

            y = o * lax.rsqrt(jnp.mean(o * o, -1, keepdims=True) + RMS_EPS) * nb_ref[0]
            ob_ref[bi, :, cs] = y * _silu(gb_ref[bi, :, cs])

    @pl.when(c == nc - 1)
    def _():
        for bi in range(bb):
            for h in range(NH):
                so_ref[0, bi, h] = st_scr[bi, h].T


def _mixer_b(proj3, s0, s_buf, lb_param, norm_b, l, C, tv, bb):
    nb, t, _ = proj3.shape
    nc = t // C
    depth = lb_param.shape[0]
    tok = lambda col: pl.BlockSpec((bb, C, D), lambda b, c: (b, c, col // D))
    return pl.pallas_call(
        functools.partial(_mixb_kernel, C=C, tv=tv, nc=nc, bb=bb, layer=l),
        grid=(nb // bb, nc),
        in_specs=[
            tok(C_QB), tok(C_FB), tok(C_IB), tok(C_GB),
            pl.BlockSpec((1, bb, NH, HD, HD), lambda b, c: (l, b, 0, 0, 0)),
            pl.BlockSpec((depth, D), lambda b, c: (0, 0)),
            _layer_spec((1, HD), l),
            pl.BlockSpec(memory_space=pl.ANY),
        ],
        out_specs=[
            pl.BlockSpec((bb, C, D), lambda b, c: (b, c, 0)),
            pl.BlockSpec((1, bb, NH, HD, HD), lambda b, c: (l, b, 0, 0, 0)),
        ],
        out_shape=[
            jax.ShapeDtypeStruct((nb, t, D), F32),
            jax.ShapeDtypeStruct(s_buf.shape, F32),
        ],
        input_output_aliases={7: 1},
        scratch_shapes=[pltpu.VMEM((bb, NH, HD, HD), F32)],
        compiler_params=_cparams("parallel", "arbitrary"),
        name="mixer_b",
    )(proj3, proj3, proj3, proj3, s0, lb_param, norm_b, s_buf)


def _post_kernel(oa_ref, ob_ref, gt_ref, x_ref, g1_ref, sc2_ref, sh2_ref, wa_ref, wb_ref, wo_ref,
                 lg_ref, lb_ref, wr_ref, br_ref, x1_ref, h2_ref, rw_ref, re_ref, *, mode, alpha):
    gates = _sigmoid(gt_ref[...])
    merged = gates[:, :D] * _dot(oa_ref[...], wa_ref[0]) + gates[:, D:] * _dot(ob_ref[...], wb_ref[0])
    y = _dot(merged, wo_ref[0])
    x1 = _layer_norm(alpha * x_ref[...] + (1.0 + _mod_val(g1_ref, mode)) * y, lg_ref[0], lb_ref[0])
    x1_ref[...] = x1
    h2 = x1 * (1.0 + _mod_val(sc2_ref, mode)) + _mod_val(sh2_ref, mode)
    h2_ref[...] = h2

    logits = _dot(h2, wr_ref[0], hi=True) + br_ref[0]
    lane = lax.broadcasted_iota(I32, logits.shape, 1)
    neg = jnp.float32(-jnp.inf)
    gl = jnp.where(lane < N_GROUPS, logits, neg)
    gmax = jnp.max(gl, -1, keepdims=True)
    grp = jnp.min(jnp.where(gl == gmax, lane, LANES), -1, keepdims=True)
    p_grp = 1.0 / jnp.sum(jnp.exp(gl - gmax), -1, keepdims=True)
    lo = N_GROUPS + grp * EPG
    el = jnp.where((lane >= lo) & (lane < lo + EPG), logits, neg)
    v1 = jnp.max(el, -1, keepdims=True)
    i1 = jnp.min(jnp.where(el == v1, lane, LANES), -1, keepdims=True)
    el2 = jnp.where(lane == i1, neg, el)
    v2 = jnp.max(el2, -1, keepdims=True)
    i2 = jnp.min(jnp.where(el2 == v2, lane, LANES), -1, keepdims=True)
    e2 = jnp.exp(v2 - v1)
    w1 = p_grp / (1.0 + e2)
    w2 = p_grp * e2 / (1.0 + e2)
    rw_ref[...] = jnp.where(lane == 0, w1, jnp.where(lane == 1, w2, 0.0))
    re_ref[...] = jnp.where(lane == 0, i1 - N_GROUPS, jnp.where(lane == 1, i2 - N_GROUPS, 0))


def _post_mixer(o_a, o_b, proj, x, mods, w_br_a, w_br_b, w_o, ln_g, ln_b, w_r, b_r, l, mode, tm,
                tiles_per_batch, alpha):
    n = x.shape[0]
    tok = pl.BlockSpec((tm, D), lambda i: (i, 0))
    lane_out = pl.BlockSpec((tm, LANES), lambda i: (i, 0))
    return pl.pallas_call(
        functools.partial(_post_kernel, mode=mode, alpha=alpha),
        grid=(n // tm,),
        in_specs=[
            tok, tok,
            pl.BlockSpec((tm, 2 * D), lambda i: (i, C_MERGE // (2 * D))),
            tok,
            _mod_spec(mode, l, tm, 2, tiles_per_batch),
            _mod_spec(mode, l, tm, 4, tiles_per_batch),
            _mod_spec(mode, l, tm, 3, tiles_per_batch),
            _layer_spec((D, D), l), _layer_spec((D, D), l), _layer_spec((D, D), l),
            _layer_spec((1, D), l), _layer_spec((1, D), l),
            _layer_spec((D, LANES), l), _layer_spec((1, LANES), l),
        ],
        out_specs=[tok, tok, lane_out, lane_out],
        out_shape=[
            jax.ShapeDtypeStruct((n, D), F32),
            jax.ShapeDtypeStruct((n, D), F32),
            jax.ShapeDtypeStruct((n, LANES), F32),
            jax.ShapeDtypeStruct((n, LANES), I32),
        ],
        compiler_params=_cparams("parallel"),
        name="post_mixer",
    )(o_a, o_b, proj, x, mods, mods, mods, w_br_a, w_br_b, w_o, ln_g, ln_b, w_r, b_r)


TM_EXP = 256


def _expert_kernel(be_ref, nblk_ref, tok_ref, h_hbm, wg_ref, wu_ref, wd_ref, ys_ref, xs, sem):
    i = pl.program_id(0)

    @pl.when(i < nblk_ref[0])
    def _():
        def row_copy(r):
            return pltpu.make_async_copy(h_hbm.at[pl.ds(tok_ref[r], 1), :], xs.at[pl.ds(r, 1), :], sem)

        def start(r, carry):
            row_copy(r).start()
            return carry

        def wait(r, carry):
            row_copy(r).wait()
            return carry

        lax.fori_loop(0, TM_EXP, start, 0)
        lax.fori_loop(0, TM_EXP, wait, 0)
        x = xs[...]
        hid = _silu(_dot(x, wg_ref[0, 0])) * _dot(x, wu_ref[0, 0])
        ys_ref[...] = _dot(hid, wd_ref[0, 0])

    @pl.when(i >= nblk_ref[0])
    def _():
        ys_ref[...] = jnp.zeros(ys_ref.shape, F32)


def _experts(h2, slot_tok, block_e, n_used, w_gate, w_up, w_down, l):
    n_rows = slot_tok.shape[0]
    n_blocks = n_rows // TM_EXP
    grid_spec = pltpu.PrefetchScalarGridSpec(
        num_scalar_prefetch=2,
        grid=(n_blocks,),
        in_specs=[
            pl.BlockSpec((TM_EXP,), lambda i, be, nb: (i,), memory_space=pltpu.SMEM),
            pl.BlockSpec(memory_space=pl.ANY),
            pl.BlockSpec((1, 1, D, D_EXP), lambda i, be, nb: (l, be[i], 0, 0)),
            pl.BlockSpec((1, 1, D, D_EXP), lambda i, be, nb: (l, be[i], 0, 0)),
            pl.BlockSpec((1, 1, D_EXP, D), lambda i, be, nb: (l, be[i], 0, 0)),
        ],
        out_specs=pl.BlockSpec((TM_EXP, D), lambda i, be, nb: (i, 0)),
        scratch_shapes=[pltpu.VMEM((TM_EXP, D), F32), pltpu.SemaphoreType.DMA],
    )
    return pl.pallas_call(
        _expert_kernel,
        grid_spec=grid_spec,
        out_shape=jax.ShapeDtypeStruct((n_rows, D), F32),
        compiler_params=_cparams("arbitrary"),
        name="experts",
    )(block_e, n_used, slot_tok, h2, w_gate, w_up, w_down)


def _combine_kernel(d0_ref, d1_ref, ys_hbm, rw_ref, x_ref, g2_ref, lg_ref, lb_ref, o_ref,
                    y0, y1, sem, *, mode, alpha, tm):
    def copies(r):
        return (pltpu.make_async_copy(ys_hbm.at[pl.ds(d0_ref[r], 1), :], y0.at[pl.ds(r, 1), :], sem),
                pltpu.make_async_copy(ys_hbm.at[pl.ds(d1_ref[r], 1), :], y1.at[pl.ds(r, 1), :], sem))

    def start(r, carry):
        a, b = copies(r)
        a.start()
        b.start()
        return carry

    def wait(r, carry):
        a, b = copies(r)
        a.wait()
        b.wait()
        return carry

    lax.fori_loop(0, tm, start, 0)
    lax.fori_loop(0, tm, wait, 0)
    rw = rw_ref[...]
    y = rw[:, 0:1] * y0[...] + rw[:, 1:2] * y1[...]
    o_ref[...] = _layer_norm(alpha * x_ref[...] + (1.0 + _mod_val(g2_ref, mode)) * y,
                             lg_ref[0], lb_ref[0])


def _combine(ys, d0, d1, rw, x1, mods, ln_g, ln_b, l, mode, tm, tiles_per_batch, alpha):
    n = x1.shape[0]
    return pl.pallas_call(
        functools.partial(_combine_kernel, mode=mode, alpha=alpha, tm=tm),
        grid=(n // tm,),
        in_specs=[
            pl.BlockSpec((tm,), lambda i: (i,), memory_space=pltpu.SMEM),
            pl.BlockSpec((tm,), lambda i: (i,), memory_space=pltpu.SMEM),
            pl.BlockSpec(memory_space=pl.ANY),
            pl.BlockSpec((tm, LANES), lambda i: (i, 0)),
            pl.BlockSpec((tm, D), lambda i: (i, 0)),
            _mod_spec(mode, l, tm, 5, tiles_per_batch),
            _layer_spec((1, D), l),
            _layer_spec((1, D), l),
        ],
        out_specs=pl.BlockSpec((tm, D), lambda i: (i, 0)),
        out_shape=jax.ShapeDtypeStruct((n, D), F32),
        scratch_shapes=[pltpu.VMEM((tm, D), F32), pltpu.VMEM((tm, D), F32), pltpu.SemaphoreType.DMA],
        compiler_params=_cparams("arbitrary"),
        name="combine",
    )(d0, d1, ys, rw, x1, mods, ln_g, ln_b)


def _route_meta(eid):
    n = eid.shape[0]
    flat = eid.reshape(-1)
    n_slots = flat.shape[0]
    onehot = (flat[:, None] == jnp.arange(N_EXP, dtype=I32)[None, :]).astype(I32)
    csum = jnp.cumsum(onehot, axis=0)
    counts = csum[-1]
    rank = jnp.take_along_axis(csum - onehot, flat[:, None], axis=1)[:, 0]
    padded = (counts + TM_EXP - 1) // TM_EXP * TM_EXP
    pad_end = jnp.cumsum(padded)
    pad_start = pad_end - padded
    dest = pad_start[flat] + rank
    n_blocks = (n_slots + N_EXP * (TM_EXP - 1)) // TM_EXP
    tok = jnp.repeat(jnp.arange(n, dtype=I32), 2)
    slot_tok = jnp.zeros((n_blocks * TM_EXP,), I32).at[dest].set(tok)
    block_start = jnp.arange(n_blocks, dtype=I32) * TM_EXP
    block_e = jnp.minimum(jnp.sum(pad_end[None, :] <= block_start[:, None], axis=1), N_EXP - 1).astype(I32)
    n_used = (pad_end[-1:] // TM_EXP).astype(I32)
    dest = dest.reshape(n, 2)
    return slot_tok, block_e, n_used, dest[:, 0], dest[:, 1]


def _pack_w_in(w_in):
    off_g_a = CONV_DIM
    off_decay = off_g_a + D
    off_q_b = off_decay + 2 * NH
    off_merge = off_q_b + 4 * D
    depth = w_in.shape[0]
    small = w_in[:, :, off_decay:off_q_b]
    pad = jnp.zeros((depth, D, LANES - 2 * NH), w_in.dtype)
    packed = jnp.concatenate([w_in[:, :, :off_decay], w_in[:, :, off_q_b:off_merge + 2 * D], small, pad], axis=2)
    return packed.astype(BF16)


def _pad_lanes(v):
    depth, k = v.shape
    return jnp.concatenate([v, jnp.zeros((depth, LANES - k), v.dtype)], axis=1).reshape(depth, 1, LANES)


def kernel(x_prompt, x_sample, state_conv, state_delta, state_hgrn, c_prompt, c_sample, ln0_g, ln0_b, w_ada, b_ada, w_in, conv_w, a_log, dt_bias, norm_a, lb_param, norm_b, w_br_a, w_br_b, w_o, ln1_g, ln1_b, w_grp, b_grp, w_rt, b_rt, w_gate_e, w_up_e, w_down_e, ln2_g, ln2_b):
    depth = w_in.shape[0]
    alpha = (2 * depth) ** 0.25
    bp, tp, _ = x_prompt.shape
    bs, ts, _ = x_sample.shape
    n_p = bp * tp
    tsp = SUBLANES
    assert ts <= tsp
    n_s = bs * tsp

    w_in_p = _pack_w_in(w_in)
    w_a16, w_b16, w_o16 = w_br_a.astype(BF16), w_br_b.astype(BF16), w_o.astype(BF16)
    w_r = jnp.concatenate([w_grp, w_rt, jnp.zeros((depth, D, LANES - N_GROUPS - N_EXP), F32)], axis=2)
    b_r = jnp.concatenate([b_grp, b_rt, jnp.zeros((depth, LANES - N_GROUPS - N_EXP), F32)], axis=1)
    b_r = b_r.reshape(depth, 1, LANES)
    a_log_p, dt_bias_p = _pad_lanes(a_log), _pad_lanes(dt_bias)
    norm_a3, norm_b3 = norm_a.reshape(depth, 1, HD), norm_b.reshape(depth, 1, HD)
    ln1_g3, ln1_b3 = ln1_g.reshape(depth, 1, D), ln1_b.reshape(depth, 1, D)
    ln2_g3, ln2_b3 = ln2_g.reshape(depth, 1, D), ln2_b.reshape(depth, 1, D)

    mods = _modulation(jnp.concatenate([c_prompt, c_sample], axis=0), w_ada, b_ada)
    mods_p = mods[:, :bp].reshape(depth, bp, 1, 6 * D)
    mods_s = jnp.repeat(mods[:, bp:], tsp, axis=1)

    xs_pad = jnp.concatenate([x_sample, jnp.zeros((bs, tsp - ts, D), x_sample.dtype)], axis=1)
    cp = math.gcd(tp, 64)
    tm_p = math.gcd(tp, 512)
    tm_s = math.gcd(n_s, 512)
    bb_s = math.gcd(bs, 4)
    state_shape = lambda nb: (depth, nb, NH, HD, HD)
    conv_shape = lambda nb: (depth, nb, CONV_W - 1, CONV_DIM)
    groups = [
        dict(mode="row", n=n_p, nb=bp, t=tp, C=cp, tv=cp, tm=tm_p, tpb=tp // tm_p, bb=1, mods=mods_p,
             conv=jnp.zeros(conv_shape(bp), F32), delta=jnp.zeros(state_shape(bp), F32),
             hgrn=jnp.zeros(state_shape(bp), F32)),
        dict(mode="tok", n=n_s, nb=bs, t=tsp, C=tsp, tv=ts, tm=tm_s, tpb=1, bb=bb_s, mods=mods_s,
             conv=state_conv, delta=state_delta, hgrn=state_hgrn),
    ]
    for g in groups:
        g["conv_o"] = jnp.zeros(conv_shape(g["nb"]), F32)
        g["delta_o"] = jnp.zeros(state_shape(g["nb"]), F32)
        g["hgrn_o"] = jnp.zeros(state_shape(g["nb"]), F32)
    xs = [_ln0(x_prompt.reshape(n_p, D), ln0_g, ln0_b, tm_p),
          _ln0(xs_pad.reshape(n_s, D), ln0_g, ln0_b, tm_s)]

    for l in range(depth):
        x1s, h2s, rws, res = [], [], [], []
        for gi, g in enumerate(groups):
            proj = _inproj(xs[gi], g["mods"], w_in_p, l, g["mode"], g["tm"], g["tpb"])
            proj3 = proj.reshape(g["nb"], g["t"], NPROJ)
            o_a, g["conv_o"], g["delta_o"] = _mixer_a(
                proj3, g["conv"], g["delta"], g["conv_o"], g["delta_o"], conv_w, a_log_p, dt_bias_p,
                norm_a3, l, g["C"], g["tv"], g["bb"])
            o_b, g["hgrn_o"] = _mixer_b(proj3, g["hgrn"], g["hgrn_o"], lb_param, norm_b3, l, g["C"],
                                        g["tv"], g["bb"])
            x1, h2, rw, re = _post_mixer(o_a.reshape(g["n"], D), o_b.reshape(g["n"], D), proj, xs[gi],
                                         g["mods"], w_a16, w_b16, w_o16, ln1_g3, ln1_b3, w_r, b_r, l,
                                         g["mode"], g["tm"] // 2, g["tpb"] * 2, alpha)
            x1s.append(x1)
            h2s.append(h2)
            rws.append(rw)
            res.append(re)
        h2_all = jnp.concatenate(h2s, axis=0)
        eid = jnp.concatenate(res, axis=0)[:, :2]
        slot_tok, block_e, n_used, d0, d1 = _route_meta(eid)
        ys = _experts(h2_all, slot_tok, block_e, n_used, w_gate_e, w_up_e, w_down_e, l)
        off = 0
        for gi, g in enumerate(groups):
            sl = slice(off, off + g["n"])
            xs[gi] = _combine(ys, d0[sl], d1[sl], rws[gi], x1s[gi], g["mods"], ln2_g3, ln2_b3, l,
                              g["mode"], g["tm"], g["tpb"], alpha)
            off += g["n"]

    y_prompt = xs[0].reshape(bp, tp, D)
    y_sample = xs[1].reshape(bs, tsp, D)[:, :ts]
    gp, gs = groups
    return (y_prompt, y_sample, gp["conv_o"], gp["delta_o"], gp["hgrn_o"],
            gs["conv_o"], gs["delta_o"], gs["hgrn_o"])
```

```python
import functools
import math

import jax
import jax.numpy as jnp
from jax import lax
from jax.experimental import pallas as pl
from jax.experimental.pallas import tpu as pltpu

F32 = jnp.float32
BF16 = jnp.bfloat16
I32 = jnp.int32

D = 1024
NH = 8
HD = 128
CONV_W = 4
CONV_DIM = 3 * D
N_GROUPS = 4
EPG = 8
N_EXP = N_GROUPS * EPG
D_EXP = 512
F_MIN = 1e-30
LN_EPS = 1e-5
RMS_EPS = 1e-6

C_QKV = 0
C_GA = 3 * D
C_QB = 4 * D
C_FB = 5 * D
C_IB = 6 * D
C_GB = 7 * D
C_MERGE = 8 * D
C_SMALL = 10 * D
NPROJ = 10 * D + 128

LANES = 128
SUBLANES = 8
VMEM_LIMIT = 48 * 1024 * 1024

HI = lax.Precision.HIGHEST


def _cparams(*sem):
    return pltpu.CompilerParams(dimension_semantics=sem, vmem_limit_bytes=VMEM_LIMIT)


def _dot(a, b, hi=False):
    if hi:
        return jnp.dot(a, b, preferred_element_type=F32, precision=HI)
    return jnp.dot(a.astype(BF16), b.astype(BF16), preferred_element_type=F32)


def _dot_nt(a, b):
    return lax.dot_general(a.astype(BF16), b.astype(BF16), (((1,), (1,)), ((), ())),
                           preferred_element_type=F32)


def _dot_tn(a, b):
    return lax.dot_general(a.astype(BF16), b.astype(BF16), (((0,), (0,)), ((), ())),
                           preferred_element_type=F32)


def _split(a):
    hi = a.astype(BF16)
    return hi, (a - hi.astype(F32)).astype(BF16)


def _dot3(a, b):
    mm = lambda x, y: jnp.dot(x, y, preferred_element_type=F32)
    return mm(a[0], b[0]) + (mm(a[0], b[1]) + mm(a[1], b[0]))


def _sigmoid(x):
    return 1.0 / (1.0 + jnp.exp(-x))


def _silu(x):
    return x * _sigmoid(x)


def _softplus(x):
    return jnp.maximum(x, 0.0) + jnp.log(1.0 + jnp.exp(-jnp.abs(x)))


def _layer_norm(x, g, b):
    mu = jnp.mean(x, -1, keepdims=True)
    xc = x - mu
    var = jnp.mean(xc * xc, -1, keepdims=True)
    return xc * lax.rsqrt(var + LN_EPS) * g + b


def _layer_spec(shape, l):
    zeros = (0,) * len(shape)
    return pl.BlockSpec((1,) + tuple(shape), lambda *_: (l,) + zeros)


def _mod_kernel(c_ref, w_ref, b_ref, o_ref):
    ca = _silu(c_ref[...])
    o_ref[0] = _dot(ca, w_ref[0]) + b_ref[0]


def _modulation(c_all, w_ada, b_ada):
    depth = w_ada.shape[0]
    nb = c_all.shape[0]
    tn = 1536
    return pl.pallas_call(
        _mod_kernel,
        grid=(depth, 6 * D // tn),
        in_specs=[
            pl.BlockSpec((nb, D), lambda l, j: (0, 0)),
            pl.BlockSpec((1, D, tn), lambda l, j: (l, 0, j)),
            pl.BlockSpec((1, 1, tn), lambda l, j: (l, 0, j)),
        ],
        out_specs=pl.BlockSpec((1, nb, tn), lambda l, j: (l, 0, j)),
        out_shape=jax.ShapeDtypeStruct((depth, nb, 6 * D), F32),
        compiler_params=_cparams("parallel", "parallel"),
        name="adaln_mod",
    )(c_all, w_ada, b_ada.reshape(depth, 1, 6 * D))


def _mod_spec(mode, l, tm, k, tiles_per_batch):
    if mode == "row":
        return pl.BlockSpec((1, 1, 1, D), lambda i, *_: (l, i // tiles_per_batch, 0, k))
    return pl.BlockSpec((1, tm, D), lambda i, *_: (l, i, k))


def _mod_val(ref, mode):
    return ref[0, 0] if mode == "row" else ref[0]


def _ln0_kernel(x_ref, g_ref, b_ref, o_ref):
    o_ref[...] = _layer_norm(x_ref[...], g_ref[...], b_ref[...])


def _ln0(x, g, b, tm):
    n = x.shape[0]
    return pl.pallas_call(
        _ln0_kernel,
        grid=(n // tm,),
        in_specs=[pl.BlockSpec((tm, D), lambda i: (i, 0)),
                  pl.BlockSpec((1, D), lambda i: (0, 0)),
                  pl.BlockSpec((1, D), lambda i: (0, 0))],
        out_specs=pl.BlockSpec((tm, D), lambda i: (i, 0)),
        out_shape=jax.ShapeDtypeStruct((n, D), F32),
        compiler_params=_cparams("parallel"),
        name="ln0",
    )(x, g.reshape(1, D), b.reshape(1, D))


def _inproj_kernel(x_ref, sc_ref, sh_ref, w_ref, o_ref, h_scr, *, mode):
    @pl.when(pl.program_id(1) == 0)
    def _():
        h = x_ref[...] * (1.0 + _mod_val(sc_ref, mode)) + _mod_val(sh_ref, mode)
        h_scr[...] = h.astype(BF16)

    o_ref[...] = jnp.dot(h_scr[...], w_ref[0], preferred_element_type=F32)


def _inproj(x, mods, w_in_p, l, mode, tm, tiles_per_batch):
    n = x.shape[0]
    tn = 1152
    return pl.pallas_call(
        functools.partial(_inproj_kernel, mode=mode),
        grid=(n // tm, NPROJ // tn),
        in_specs=[
            pl.BlockSpec((tm, D), lambda i, j: (i, 0)),
            _mod_spec(mode, l, tm, 1, tiles_per_batch),
            _mod_spec(mode, l, tm, 0, tiles_per_batch),
            pl.BlockSpec((1, D, tn), lambda i, j: (l, 0, j)),
        ],
        out_specs=pl.BlockSpec((tm, tn), lambda i, j: (i, j)),
        out_shape=jax.ShapeDtypeStruct((n, NPROJ), F32),
        scratch_shapes=[pltpu.VMEM((tm, D), BF16)],
        compiler_params=_cparams("parallel", "arbitrary"),
        name="inproj",
    )(x, mods, mods, w_in_p)


def _mixa_kernel(qkv_ref, ga_ref, sm_ref, cst_ref, s0_ref, cw_ref, alog_ref, dtb_ref, na_ref,
                 cbuf_ref, sbuf_ref, oa_ref, cso_ref, so_ref, xbuf, s_scr, *, C, tv, nc, bb):
    del cbuf_ref, sbuf_ref
    c = pl.program_id(1)
    row = lax.broadcasted_iota(I32, (C, 1), 0)
    ri = lax.broadcasted_iota(I32, (C, C), 0)
    ci = lax.broadcasted_iota(I32, (C, C), 1)
    causal = ri >= ci
    strict = ri > ci
    eye = (ri == ci).astype(F32)
    tril = causal.astype(F32)
    hist = SUBLANES - (CONV_W - 1)

    @pl.when(c == 0)
    def _():
        for bi in range(bb):
            xbuf[bi, 0:SUBLANES, :] = jnp.zeros((SUBLANES, CONV_DIM), F32)
            xbuf[bi, hist:SUBLANES, :] = cst_ref[0, bi]
            s_scr[bi] = s0_ref[0, bi]

    pre = []
    for bi in range(bb):
        xbuf[bi, SUBLANES:SUBLANES + C, :] = qkv_ref[bi]
        sm = sm_ref[bi]
        g_all = -jnp.exp(alog_ref[0]) * _softplus(sm + dtb_ref[0])
        beta_all = _sigmoid(sm)
        if tv < C:
            g_all = jnp.where(row < tv, g_all, 0.0)
            beta_all = jnp.where(row < tv, beta_all, 0.0)
        gcum_all = _dot(tril, g_all, hi=True)
        pre.append((gcum_all, gcum_all.T, jnp.exp(gcum_all), beta_all))

    n_nil = min(C, tv)
    n_double = max(int(math.ceil(math.log2(n_nil))) - 1, 0)
    units = [(bi, h) for bi in range(bb) for h in range(NH)]

    st = []
    for bi, h in units:
        gcum_all, gcum_t, eg_all, beta_all = pre[bi]
        conv = []
        for part in range(3):
            col = slice(part * D + h * HD, part * D + (h + 1) * HD)
            acc = None
            for j in range(CONV_W):
                term = xbuf[bi, hist + j:hist + j + C, col] * cw_ref[0, j:j + 1, col]
                acc = term if acc is None else acc + term
            conv.append(_silu(acc))
        q, k, v = conv
        q = q * lax.rsqrt(jnp.sum(q * q, -1, keepdims=True) + RMS_EPS) * (HD ** -0.5)
        k = k * lax.rsqrt(jnp.sum(k * k, -1, keepdims=True) + RMS_EPS)
        gc = gcum_all[:, h:h + 1]
        gr = gcum_t[h:h + 1, :]
        bt = beta_all[:, NH + h:NH + h + 1]
        eg = eg_all[:, h:h + 1]
        glast = gcum_all[tv - 1:tv, h:h + 1]
        dec = jnp.where(causal, jnp.exp(jnp.where(causal, gc - gr, 0.0)), 0.0)
        kb = k * bt
        st.append(dict(
            a=-jnp.where(strict, _dot_nt(kb, k) * dec, 0.0),
            rhs=jnp.concatenate([kb * eg, v * bt], axis=1),
            qk=jnp.where(causal, _dot_nt(q, k) * dec, 0.0),
            qe=q * eg,
            kg=k * jnp.exp(glast - gc),
            eglast=jnp.exp(glast),
        ))

    ps = [eye + u["a"] for u in st]
    qsp = [_split(u["a"]) for u in st]
    for _ in range(n_double):
        qs = [_dot3(s2, s2) for s2 in qsp]
        qsp = [_split(x) for x in qs]
        ps = [p + _dot3(_split(p), s2) for p, s2 in zip(ps, qsp)]
    sols = [_dot3(_split(p), _split(u["rhs"])) for p, u in zip(ps, st)]

    for (bi, h), u, sol in zip(units, st, sols):
        cs = slice(h * HD, (h + 1) * HD)
        s = s_scr[bi, h]
        ws = _dot(jnp.concatenate([sol[:, :HD], u["qe"]], axis=0), s)
        vn = sol[:, HD:] - ws[:C]
        o = ws[C:] + _dot(u["qk"], vn)
        s_scr[bi, h] = u["eglast"] * s + _dot_tn(u["kg"], vn)
        y = o * lax.rsqrt(jnp.mean(o * o, -1, keepdims=True) + RMS_EPS) * na_ref[0]
        oa_ref[bi, :, cs] = y * _silu(ga_ref[bi, :, cs])

    if nc > 1:
        for bi in range(bb):
            xbuf[bi, 0:SUBLANES, :] = xbuf[bi, C:C + SUBLANES, :]

    @pl.when(c == nc - 1)
    def _():
        for bi in range(bb):
            if nc > 1:
                cso_ref[0, bi] = xbuf[bi, hist:SUBLANES, :]
            else:
                cso_ref[0, bi] = xbuf[bi, SUBLANES + tv - (CONV_W - 1):SUBLANES + tv, :]
            so_ref[0, bi] = s_scr[bi]


def _mixer_a(proj3, conv_state, s0, conv_buf, s_buf, conv_w, a_log_p, dt_bias_p, norm_a, l, C, tv, bb):
    nb, t, _ = proj3.shape
    nc = t // C
    assert nc == 1 or tv == C
    return pl.pallas_call(
        functools.partial(_mixa_kernel, C=C, tv=tv, nc=nc, bb=bb),
        grid=(nb // bb, nc),
        in_specs=[
            pl.BlockSpec((bb, C, CONV_DIM), lambda b, c: (b, c, C_QKV // CONV_DIM)),
            pl.BlockSpec((bb, C, D), lambda b, c: (b, c, C_GA // D)),
            pl.BlockSpec((bb, C, LANES), lambda b, c: (b, c, C_SMALL // LANES)),
            pl.BlockSpec((1, bb, CONV_W - 1, CONV_DIM), lambda b, c: (l, b, 0, 0)),
            pl.BlockSpec((1, bb, NH, HD, HD), lambda b, c: (l, b, 0, 0, 0)),
            _layer_spec((CONV_W, CONV_DIM), l),
            _layer_spec((1, LANES), l),
            _layer_spec((1, LANES), l),
            _layer_spec((1, HD), l),
            pl.BlockSpec(memory_space=pl.ANY),
            pl.BlockSpec(memory_space=pl.ANY),
        ],
        out_specs=[
            pl.BlockSpec((bb, C, D), lambda b, c: (b, c, 0)),
            pl.BlockSpec((1, bb, CONV_W - 1, CONV_DIM), lambda b, c: (l, b, 0, 0)),
            pl.BlockSpec((1, bb, NH, HD, HD), lambda b, c: (l, b, 0, 0, 0)),
        ],
        out_shape=[
            jax.ShapeDtypeStruct((nb, t, D), F32),
            jax.ShapeDtypeStruct(conv_buf.shape, F32),
            jax.ShapeDtypeStruct(s_buf.shape, F32),
        ],
        input_output_aliases={9: 1, 10: 2},
        scratch_shapes=[pltpu.VMEM((bb, C + 2 * SUBLANES, CONV_DIM), F32),
                        pltpu.VMEM((bb, NH, HD, HD), F32)],
        compiler_params=_cparams("parallel", "arbitrary"),
        name="mixer_a",
    )(proj3, proj3, proj3, conv_state, s0, conv_w, a_log_p, dt_bias_p, norm_a, conv_buf, s_buf)


DIAG = SUBLANES


def _mixb_kernel(qb_ref, fb_ref, ib_ref, gb_ref, s0_ref, lbp_ref, nb_ref, sbuf_ref,
                 ob_ref, so_ref, st_scr, *, C, tv, nc, bb, layer):
    del sbuf_ref
    c = pl.program_id(1)
    row = lax.broadcasted_iota(I32, (C, 1), 0)
    ri = lax.broadcasted_iota(I32, (C, C), 0)
    ci = lax.broadcasted_iota(I32, (C, C), 1)
    tril = (ri >= ci).astype(F32)
    row8 = lax.broadcasted_iota(I32, (DIAG, 1), 0)

    @pl.when(c == 0)
    def _():
        for bi in range(bb):
            for h in range(NH):
                st_scr[bi, h] = s0_ref[0, bi, h].T

    lbp = lbp_ref[...]
    e = jnp.exp(lbp - jnp.max(lbp, axis=0, keepdims=True))
    psm = e / jnp.sum(e, axis=0, keepdims=True)
    lb = jnp.zeros((1, D), F32)
    for j in range(1, layer + 1):
        lb = lb + psm[j:j + 1, :]

    levels = []
    lv = DIAG
    while lv < C:
        levels.append(lv)
        lv *= 2

    for bi in range(bb):
        z = fb_ref[bi]
        f = lb + (1.0 - lb) * _sigmoid(z)
        lf_all = jnp.log(jnp.maximum(f, F_MIN))
        kk_all = (1.0 - lb) * _sigmoid(-z)
        if tv < C:
            lf_all = jnp.where(row < tv, lf_all, 0.0)
            kk_all = jnp.where(row < tv, kk_all, 0.0)
        b_all = _dot(tril, lf_all, hi=True)

        for h in range(NH):
            cs = slice(h * HD, (h + 1) * HD)
            q = _silu(qb_ref[bi, :, cs])
            kk = kk_all[:, cs]
            iv = ib_ref[bi, :, cs]
            b = b_all[:, cs]
            blast = b[tv - 1:tv, :]

            amat = None
            for lv in levels:
                pieces = []
                for j in range(C // lv):
                    blk = b[j * lv:(j + 1) * lv, :]
                    if j % 2 == 1:
                        pieces.append(blk - b[j * lv:j * lv + 1, :])
                    else:
                        pieces.append(b[(j + 1) * lv:(j + 1) * lv + 1, :] - blk)
                ex = jnp.exp(jnp.concatenate(pieces, axis=0))
                lower = ((row // lv) % 2) == 1
                qs = jnp.where(lower, q * ex, 0.0)
                ks = jnp.where(lower, 0.0, kk * ex)
                blk_a = jnp.where((ri // (2 * lv)) == (ci // (2 * lv)), _dot_nt(qs, ks), 0.0)
                amat = blk_a if amat is None else amat + blk_a

            od = []
            for j in range(C // DIAG):
                rs = slice(j * DIAG, (j + 1) * DIAG)
                bd, qq, k8, i8 = b[rs, :], q[rs, :], kk[rs, :], iv[rs, :]
                acc = jnp.zeros((DIAG, HD), F32)
                for s in range(DIAG):
                    ok = row8 >= s
                    ex = jnp.exp(jnp.where(ok, bd - bd[s:s + 1, :], 0.0))
                    zz = jnp.where(ok, qq * ex * k8[s:s + 1, :], 0.0)
                    acc = acc + jnp.sum(zz, -1, keepdims=True) * i8[s:s + 1, :]
                od.append(acc)
            o = od[0] if len(od) == 1 else jnp.concatenate(od, axis=0)
            if amat is not None:
                o = o + _dot(amat, iv)

            st = st_scr[bi, h]
            o = o + _dot_nt(q * jnp.exp(b), st)
            kg = kk * jnp.exp(blast - b)
            st_scr[bi, h] = st * jnp.exp(blast) + _dot_tn(iv, kg)

            y = o * lax.rsqrt(jnp.mean(o * o, -1, keepdims=True) + RMS_EPS) * nb_ref[0]
            ob_ref[bi, :, cs] = y * _silu(gb_ref[bi, :, cs])

    @pl.when(c == nc - 1)
    def _():
        for bi in range(bb):
            for h in range(NH):
                so_ref[0, bi, h] = st_scr[bi, h].T


def _mixer_b(proj3, s0, s_buf, lb_param, norm_b, l, C, tv, bb):
    nb, t, _ = proj3.shape
    nc = t // C
    depth = lb_param.shape[0]
    tok = lambda col: pl.BlockSpec((bb, C, D), lambda b, c: (b, c, col // D))
    return pl.pallas_call(
        functools.partial(_mixb_kernel, C=C, tv=tv, nc=nc, bb=bb, layer=l),
        grid=(nb // bb, nc),
        in_specs=[
            tok(C_QB), tok(C_FB), tok(C_IB), tok(C_GB),
            pl.BlockSpec((1, bb, NH, HD, HD), lambda b, c: (l, b, 0, 0, 0)),
            pl.BlockSpec((depth, D), lambda b, c: (0, 0)),
            _layer_spec((1, HD), l),
            pl.BlockSpec(memory_space=pl.ANY),
        ],
        out_specs=[
            pl.BlockSpec((bb, C, D), lambda b, c: (b, c, 0)),
            pl.BlockSpec((1, bb, NH, HD, HD), lambda b, c: (l, b, 0, 0, 0)),
        ],
        out_shape=[
            jax.ShapeDtypeStruct((nb, t, D), F32),
            jax.ShapeDtypeStruct(s_buf.shape, F32),
        ],
        input_output_aliases={7: 1},
        scratch_shapes=[pltpu.VMEM((bb, NH, HD, HD), F32)],
        compiler_params=_cparams("parallel", "arbitrary"),
        name="mixer_b",
    )(proj3, proj3, proj3, proj3, s0, lb_param, norm_b, s_buf)


def _post_kernel(oa_ref, ob_ref, gt_ref, x_ref, g1_ref, sc2_ref, sh2_ref, wa_ref, wb_ref, wo_ref,
                 lg_ref, lb_ref, wr_ref, br_ref, x1_ref, h2_ref, rw_ref, re_ref, *, mode, alpha):
    gates = _sigmoid(gt_ref[...])
    merged = gates[:, :D] * _dot(oa_ref[...], wa_ref[0]) + gates[:, D:] * _dot(ob_ref[...], wb_ref[0])
    y = _dot(merged, wo_ref[0])
    x1 = _layer_norm(alpha * x_ref[...] + (1.0 + _mod_val(g1_ref, mode)) * y, lg_ref[0], lb_ref[0])
    x1_ref[...] = x1
    h2 = x1 * (1.0 + _mod_val(sc2_ref, mode)) + _mod_val(sh2_ref, mode)
    h2_ref[...] = h2

    logits = _dot(h2, wr_ref[0], hi=True) + br_ref[0]
    lane = lax.broadcasted_iota(I32, logits.shape, 1)
    neg = jnp.float32(-jnp.inf)
    gl = jnp.where(lane < N_GROUPS, logits, neg)
    gmax = jnp.max(gl, -1, keepdims=True)
    grp = jnp.min(jnp.where(gl == gmax, lane, LANES), -1, keepdims=True)
    p_grp = 1.0 / jnp.sum(jnp.exp(gl - gmax), -1, keepdims=True)
    lo = N_GROUPS + grp * EPG
    el = jnp.where((lane >= lo) & (lane < lo + EPG), logits, neg)
    v1 = jnp.max(el, -1, keepdims=True)
    i1 = jnp.min(jnp.where(el == v1, lane, LANES), -1, keepdims=True)
    el2 = jnp.where(lane == i1, neg, el)
    v2 = jnp.max(el2, -1, keepdims=True)
    i2 = jnp.min(jnp.where(el2 == v2, lane, LANES), -1, keepdims=True)
    e2 = jnp.exp(v2 - v1)
    w1 = p_grp / (1.0 + e2)
    w2 = p_grp * e2 / (1.0 + e2)
    rw_ref[...] = jnp.where(lane == 0, w1, jnp.where(lane == 1, w2, 0.0))
    re_ref[...] = jnp.where(lane == 0, i1 - N_GROUPS, jnp.where(lane == 1, i2 - N_GROUPS, 0))


def _post_mixer(o_a, o_b, proj, x, mods, w_br_a, w_br_b, w_o, ln_g, ln_b, w_r, b_r, l, mode, tm,
                tiles_per_batch, alpha):
    n = x.shape[0]
    tok = pl.BlockSpec((tm, D), lambda i: (i, 0))
    lane_out = pl.BlockSpec((tm, LANES), lambda i: (i, 0))
    return pl.pallas_call(
        functools.partial(_post_kernel, mode=mode, alpha=alpha),
        grid=(n // tm,),
        in_specs=[
            tok, tok,
            pl.BlockSpec((tm, 2 * D), lambda i: (i, C_MERGE // (2 * D))),
            tok,
            _mod_spec(mode, l, tm, 2, tiles_per_batch),
            _mod_spec(mode, l, tm, 4, tiles_per_batch),
            _mod_spec(mode, l, tm, 3, tiles_per_batch),
            _layer_spec((D, D), l), _layer_spec((D, D), l), _layer_spec((D, D), l),
            _layer_spec((1, D), l), _layer_spec((1, D), l),
            _layer_spec((D, LANES), l), _layer_spec((1, LANES), l),
        ],
        out_specs=[tok, tok, lane_out, lane_out],
        out_shape=[
            jax.ShapeDtypeStruct((n, D), F32),
            jax.ShapeDtypeStruct((n, D), F32),
            jax.ShapeDtypeStruct((n, LANES), F32),
            jax.ShapeDtypeStruct((n, LANES), I32),
        ],
        compiler_params=_cparams("parallel"),
        name="post_mixer",
    )(o_a, o_b, proj, x, mods, mods, mods, w_br_a, w_br_b, w_o, ln_g, ln_b, w_r, b_r)


TM_EXP = 256


def _dispatch_kernel(d0_ref, d1_ref, h_ref, xs_in, xs_out, sem, *, tm):
    del xs_in

    def copies(r):
        src = h_ref.at[pl.ds(r, 1), :]
        return (pltpu.make_async_copy(src, xs_out.at[pl.ds(d0_ref[r], 1), :], sem),
                pltpu.make_async_copy(src, xs_out.at[pl.ds(d1_ref[r], 1), :], sem))

    def start(r, carry):
        a, b = copies(r)
        a.start()
        b.start()
        return carry

    def wait(r, carry):
        a, b = copies(r)
        a.wait()
        b.wait()
        return carry

    lax.fori_loop(0, tm, start, 0, unroll=4)
    lax.fori_loop(0, tm, wait, 0, unroll=4)


def _dispatch(h2, d0, d1, xs_buf, tm):
    n = h2.shape[0]
    return pl.pallas_call(
        functools.partial(_dispatch_kernel, tm=tm),
        grid=(n // tm,),
        in_specs=[
            pl.BlockSpec((tm,), lambda i: (i,), memory_space=pltpu.SMEM),
            pl.BlockSpec((tm,), lambda i: (i,), memory_space=pltpu.SMEM),
            pl.BlockSpec((tm, D), lambda i: (i, 0)),
            pl.BlockSpec(memory_space=pl.ANY),
        ],
        out_specs=pl.BlockSpec(memory_space=pl.ANY),
        out_shape=jax.ShapeDtypeStruct(xs_buf.shape, F32),
        input_output_aliases={3: 0},
        scratch_shapes=[pltpu.SemaphoreType.DMA],
        compiler_params=_cparams("arbitrary"),
        name="dispatch",
    )(d0, d1, h2, xs_buf)


def _expert_kernel(be_ref, nblk_ref, xs_ref, wg_ref, wu_ref, wd_ref, ys_ref):
    i = pl.program_id(0)

    @pl.when(i < nblk_ref[0])
    def _():
        x = xs_ref[...]
        hid = _silu(_dot(x, wg_ref[0, 0])) * _dot(x, wu_ref[0, 0])
        ys_ref[...] = _dot(hid, wd_ref[0, 0])

    @pl.when(i >= nblk_ref[0])
    def _():
        ys_ref[...] = jnp.zeros(ys_ref.shape, F32)


def _experts(xs, block_e, n_used, w_gate, w_up, w_down, l):
    n_rows = xs.shape[0]
    n_blocks = n_rows // TM_EXP
    grid_spec = pltpu.PrefetchScalarGridSpec(
        num_scalar_prefetch=2,
        grid=(n_blocks,),
        in_specs=[
            pl.BlockSpec((TM_EXP, D), lambda i, be, nb: (i, 0)),
            pl.BlockSpec((1, 1, D, D_EXP), lambda i, be, nb: (l, be[i], 0, 0)),
            pl.BlockSpec((1, 1, D, D_EXP), lambda i, be, nb: (l, be[i], 0, 0)),
            pl.BlockSpec((1, 1, D_EXP, D), lambda i, be, nb: (l, be[i], 0, 0)),
        ],
        out_specs=pl.BlockSpec((TM_EXP, D), lambda i, be, nb: (i, 0)),
    )
    return pl.pallas_call(
        _expert_kernel,
        grid_spec=grid_spec,
        out_shape=jax.ShapeDtypeStruct((n_rows, D), F32),
        compiler_params=_cparams("arbitrary"),
        name="experts",
    )(block_e, n_used, xs, w_gate, w_up, w_down)


def _combine_kernel(d0_ref, d1_ref, ys_hbm, rw_ref, x_ref, g2_ref, lg_ref, lb_ref, o_ref,
                    y0, y1, sem, *, mode, alpha, tm):
    def copies(r):
        return (pltpu.make_async_copy(ys_hbm.at[pl.ds(d0_ref[r], 1), :], y0.at[pl.ds(r, 1), :], sem),
                pltpu.make_async_copy(ys_hbm.at[pl.ds(d1_ref[r], 1), :], y1.at[pl.ds(r, 1), :], sem))

    def start(r, carry):
        a, b = copies(r)
        a.start()
        b.start()
        return carry

    def wait(r, carry):
        a, b = copies(r)
        a.wait()
        b.wait()
        return carry

    lax.fori_loop(0, tm, start, 0, unroll=4)
    lax.fori_loop(0, tm, wait, 0, unroll=4)
    rw = rw_ref[...]
    y = rw[:, 0:1] * y0[...] + rw[:, 1:2] * y1[...]
    o_ref[...] = _layer_norm(alpha * x_ref[...] + (1.0 + _mod_val(g2_ref, mode)) * y,
                             lg_ref[0], lb_ref[0])


def _combine(ys, d0, d1, rw, x1, mods, ln_g, ln_b, l, mode, tm, tiles_per_batch, alpha):
    n = x1.shape[0]
    return pl.pallas_call(
        functools.partial(_combine_kernel, mode=mode, alpha=alpha, tm=tm),
        grid=(n // tm,),
        in_specs=[
            pl.BlockSpec((tm,), lambda i: (i,), memory_space=pltpu.SMEM),
            pl.BlockSpec((tm,), lambda i: (i,), memory_space=pltpu.SMEM),
            pl.BlockSpec(memory_space=pl.ANY),
            pl.BlockSpec((tm, LANES), lambda i: (i, 0)),
            pl.BlockSpec((tm, D), lambda i: (i, 0)),
            _mod_spec(mode, l, tm, 5, tiles_per_batch),
            _layer_spec((1, D), l),
            _layer_spec((1, D), l),
        ],
        out_specs=pl.BlockSpec((tm, D), lambda i: (i, 0)),
        out_shape=jax.ShapeDtypeStruct((n, D), F32),
        scratch_shapes=[pltpu.VMEM((tm, D), F32), pltpu.VMEM((tm, D), F32), pltpu.SemaphoreType.DMA],
        compiler_params=_cparams("arbitrary"),
        name="combine",
    )(d0, d1, ys, rw, x1, mods, ln_g, ln_b)


def _route_meta(eid):
    n = eid.shape[0]
    flat = eid.reshape(-1)
    n_slots = flat.shape[0]
    onehot = (flat[:, None] == jnp.arange(N_EXP, dtype=I32)[None, :]).astype(I32)
    csum = jnp.cumsum(onehot, axis=0)
    counts = csum[-1]
    rank = jnp.take_along_axis(csum - onehot, flat[:, None], axis=1)[:, 0]
    padded = (counts + TM_EXP - 1) // TM_EXP * TM_EXP
    pad_end = jnp.cumsum(padded)
    pad_start = pad_end - padded
    dest = pad_start[flat] + rank
    n_blocks = (n_slots + N_EXP * (TM_EXP - 1)) // TM_EXP
    block_start = jnp.arange(n_blocks, dtype=I32) * TM_EXP
    block_e = jnp.minimum(jnp.sum(pad_end[None, :] <= block_start[:, None], axis=1), N_EXP - 1).astype(I32)
    n_used = (pad_end[-1:] // TM_EXP).astype(I32)
    dest = dest.reshape(n, 2)
    return n_blocks * TM_EXP, block_e, n_used, dest[:, 0], dest[:, 1]


def _pack_w_in(w_in):
    off_g_a = CONV_DIM
    off_decay = off_g_a + D
    off_q_b = off_decay + 2 * NH
    off_merge = off_q_b + 4 * D
    depth = w_in.shape[0]
    small = w_in[:, :, off_decay:off_q_b]
    pad = jnp.zeros((depth, D, LANES - 2 * NH), w_in.dtype)
    packed = jnp.concatenate([w_in[:, :, :off_decay], w_in[:, :, off_q_b:off_merge + 2 * D], small, pad], axis=2)
    return packed.astype(BF16)


def _pad_lanes(v):
    depth, k = v.shape
    return jnp.concatenate([v, jnp.zeros((depth, LANES - k), v.dtype)], axis=1).reshape(depth, 1, LANES)


def kernel(x_prompt, x_sample, state_conv, state_delta, state_hgrn, c_prompt, c_sample, ln0_g, ln0_b, w_ada, b_ada, w_in, conv_w, a_log, dt_bias, norm_a, lb_param, norm_b, w_br_a, w_br_b, w_o, ln1_g, ln1_b, w_grp, b_grp, w_rt, b_rt, w_gate_e, w_up_e, w_down_e, ln2_g, ln2_b):
    depth = w_in.shape[0]
    alpha = (2 * depth) ** 0.25
    bp, tp, _ = x_prompt.shape
    bs, ts, _ = x_sample.shape
    n_p = bp * tp
    tsp = SUBLANES
    assert ts <= tsp
    n_s = bs * tsp

    w_in_p = _pack_w_in(w_in)
    w_a16, w_b16, w_o16 = w_br_a.astype(BF16), w_br_b.astype(BF16), w_o.astype(BF16)
    w_r = jnp.concatenate([w_grp, w_rt, jnp.zeros((depth, D, LANES - N_GROUPS - N_EXP), F32)], axis=2)
    b_r = jnp.concatenate([b_grp, b_rt, jnp.zeros((depth, LANES - N_GROUPS - N_EXP), F32)], axis=1)
    b_r = b_r.reshape(depth, 1, LANES)
    a_log_p, dt_bias_p = _pad_lanes(a_log), _pad_lanes(dt_bias)
    norm_a3, norm_b3 = norm_a.reshape(depth, 1, HD), norm_b.reshape(depth, 1, HD)
    ln1_g3, ln1_b3 = ln1_g.reshape(depth, 1, D), ln1_b.reshape(depth, 1, D)
    ln2_g3, ln2_b3 = ln2_g.reshape(depth, 1, D), ln2_b.reshape(depth, 1, D)

    mods = _modulation(jnp.concatenate([c_prompt, c_sample], axis=0), w_ada, b_ada)
    mods_p = mods[:, :bp].reshape(depth, bp, 1, 6 * D)
    mods_s = jnp.repeat(mods[:, bp:], tsp, axis=1)

    xs_pad = jnp.concatenate([x_sample, jnp.zeros((bs, tsp - ts, D), x_sample.dtype)], axis=1)
    cp = math.gcd(tp, 64)
    tm_p = math.gcd(tp, 512)
    tm_s = math.gcd(n_s, 512)
    bb_s = math.gcd(bs, 4)
    state_shape = lambda nb: (depth, nb, NH, HD, HD)
    conv_shape = lambda nb: (depth, nb, CONV_W - 1, CONV_DIM)
    groups = [
        dict(mode="row", n=n_p, nb=bp, t=tp, C=cp, tv=cp, tm=tm_p, tpb=tp // tm_p, bb=1, mods=mods_p,
             conv=jnp.zeros(conv_shape(bp), F32), delta=jnp.zeros(state_shape(bp), F32),
             hgrn=jnp.zeros(state_shape(bp), F32)),
        dict(mode="tok", n=n_s, nb=bs, t=tsp, C=tsp, tv=ts, tm=tm_s, tpb=1, bb=bb_s, mods=mods_s,
             conv=state_conv, delta=state_delta, hgrn=state_hgrn),
    ]
    for g in groups:
        g["conv_o"] = jnp.zeros(conv_shape(g["nb"]), F32)
        g["delta_o"] = jnp.zeros(state_shape(g["nb"]), F32)
        g["hgrn_o"] = jnp.zeros(state_shape(g["nb"]), F32)
    xs = [_ln0(x_prompt.reshape(n_p, D), ln0_g, ln0_b, tm_p),
          _ln0(xs_pad.reshape(n_s, D), ln0_g, ln0_b, tm_s)]

    for l in range(depth):
        x1s, h2s, rws, res = [], [], [], []
        for gi, g in enumerate(groups):
            tm_in = math.gcd(g["t"] if g["mode"] == "row" else g["n"], 1024)
            proj = _inproj(xs[gi], g["mods"], w_in_p, l, g["mode"], tm_in, g["t"] // tm_in)
            proj3 = proj.reshape(g["nb"], g["t"], NPROJ)
            o_a, g["conv_o"], g["delta_o"] = _mixer_a(
                proj3, g["conv"], g["delta"], g["conv_o"], g["delta_o"], conv_w, a_log_p, dt_bias_p,
                norm_a3, l, g["C"], g["tv"], g["bb"])
            o_b, g["hgrn_o"] = _mixer_b(proj3, g["hgrn"], g["hgrn_o"], lb_param, norm_b3, l, g["C"],
                                        g["tv"], g["bb"])
            x1, h2, rw, re = _post_mixer(o_a.reshape(g["n"], D), o_b.reshape(g["n"], D), proj, xs[gi],
                                         g["mods"], w_a16, w_b16, w_o16, ln1_g3, ln1_b3, w_r, b_r, l,
                                         g["mode"], g["tm"] // 2, g["tpb"] * 2, alpha)
            x1s.append(x1)
            h2s.append(h2)
            rws.append(rw)
            res.append(re)
        eid = jnp.concatenate([r[:, :2] for r in res], axis=0)
        n_rows, block_e, n_used, d0, d1 = _route_meta(eid)
        xs_sorted = jnp.zeros((n_rows, D), F32)
        off = 0
        for gi, g in enumerate(groups):
            sl = slice(off, off + g["n"])
            xs_sorted = _dispatch(h2s[gi], d0[sl], d1[sl], xs_sorted, g["tm"])
            off += g["n"]
        ys = _experts(xs_sorted, block_e, n_used, w_gate_e, w_up_e, w_down_e, l)
        off = 0
        for gi, g in enumerate(groups):
            sl = slice(off, off + g["n"])
            xs[gi] = _combine(ys, d0[sl], d1[sl], rws[gi], x1s[gi], g["mods"], ln2_g3, ln2_b3, l,
                              g["mode"], g["tm"], g["tpb"], alpha)
            off += g["n"]

    y_prompt = xs[0].reshape(bp, tp, D)
    y_sample = xs[1].reshape(bs, tsp, D)[:, :ts]
    gp, gs = groups
    return (y_prompt, y_sample, gp["conv_o"], gp["delta_o"], gp["hgrn_o"],
            gs["conv_o"], gs["delta_o"], gs["hgrn_o"])
```

```python
import functools
import math

import jax
import jax.numpy as jnp
from jax import lax
from jax.experimental import pallas as pl
from jax.experimental.pallas import tpu as pltpu

F32 = jnp.float32
BF16 = jnp.bfloat16
I32 = jnp.int32

D = 1024
NH = 8
HD = 128
CONV_W = 4
CONV_DIM = 3 * D
N_GROUPS = 4
EPG = 8
N_EXP = N_GROUPS * EPG
D_EXP = 512
F_MIN = 1e-30
LN_EPS = 1e-5
RMS_EPS = 1e-6

C_QKV = 0
C_GA = 3 * D
C_QB = 4 * D
C_FB = 5 * D
C_IB = 6 * D
C_GB = 7 * D
C_MERGE = 8 * D
C_SMALL = 10 * D
NPROJ = 10 * D + 128

LANES = 128
SUBLANES = 8
VMEM_LIMIT = 48 * 1024 * 1024

HI = lax.Precision.HIGHEST


def _cparams(*sem):
    return pltpu.CompilerParams(dimension_semantics=sem, vmem_limit_bytes=VMEM_LIMIT)


def _dot(a, b, hi=False):
    if hi:
        return jnp.dot(a, b, preferred_element_type=F32, precision=HI)
    return jnp.dot(a.astype(BF16), b.astype(BF16), preferred_element_type=F32)


def _dot_nt(a, b):
    return lax.dot_general(a.astype(BF16), b.astype(BF16), (((1,), (1,)), ((), ())),
                           preferred_element_type=F32)


def _dot_tn(a, b):
    return lax.dot_general(a.astype(BF16), b.astype(BF16), (((0,), (0,)), ((), ())),
                           preferred_element_type=F32)


def _split(a):
    hi = a.astype(BF16)
    return hi, (a - hi.astype(F32)).astype(BF16)


def _dot3(a, b):
    mm = lambda x, y: jnp.dot(x, y, preferred_element_type=F32)
    return mm(a[0], b[0]) + (mm(a[0], b[1]) + mm(a[1], b[0]))


def _sigmoid(x):
    return 1.0 / (1.0 + jnp.exp(-x))


def _silu(x):
    return x * _sigmoid(x)


def _softplus(x):
    return jnp.maximum(x, 0.0) + jnp.log(1.0 + jnp.exp(-jnp.abs(x)))


def _layer_norm(x, g, b):
    mu = jnp.mean(x, -1, keepdims=True)
    xc = x - mu
    var = jnp.mean(xc * xc, -1, keepdims=True)
    return xc * lax.rsqrt(var + LN_EPS) * g + b


def _layer_spec(shape, l):
    zeros = (0,) * len(shape)
    return pl.BlockSpec((1,) + tuple(shape), lambda *_: (l,) + zeros)


def _mod_kernel(c_ref, w_ref, b_ref, o_ref):
    ca = _silu(c_ref[...])
    o_ref[0] = _dot(ca, w_ref[0]) + b_ref[0]


def _modulation(c_all, w_ada, b_ada):
    depth = w_ada.shape[0]
    nb = c_all.shape[0]
    tn = 1536
    return pl.pallas_call(
        _mod_kernel,
        grid=(depth, 6 * D // tn),
        in_specs=[
            pl.BlockSpec((nb, D), lambda l, j: (0, 0)),
            pl.BlockSpec((1, D, tn), lambda l, j: (l, 0, j)),
            pl.BlockSpec((1, 1, tn), lambda l, j: (l, 0, j)),
        ],
        out_specs=pl.BlockSpec((1, nb, tn), lambda l, j: (l, 0, j)),
        out_shape=jax.ShapeDtypeStruct((depth, nb, 6 * D), F32),
        compiler_params=_cparams("parallel", "parallel"),
        name="adaln_mod",
    )(c_all, w_ada, b_ada.reshape(depth, 1, 6 * D))


def _mod_spec(mode, l, tm, k, tiles_per_batch):
    if mode == "row":
        return pl.BlockSpec((1, 1, 1, D), lambda i, *_: (l, i // tiles_per_batch, 0, k))
    return pl.BlockSpec((1, tm, D), lambda i, *_: (l, i, k))


def _mod_val(ref, mode):
    return ref[0, 0] if mode == "row" else ref[0]


def _ln0_kernel(x_ref, g_ref, b_ref, o_ref):
    o_ref[...] = _layer_norm(x_ref[...], g_ref[...], b_ref[...])


def _ln0(x, g, b, tm):
    n = x.shape[0]
    return pl.pallas_call(
        _ln0_kernel,
        grid=(n // tm,),
        in_specs=[pl.BlockSpec((tm, D), lambda i: (i, 0)),
                  pl.BlockSpec((1, D), lambda i: (0, 0)),
                  pl.BlockSpec((1, D), lambda i: (0, 0))],
        out_specs=pl.BlockSpec((tm, D), lambda i: (i, 0)),
        out_shape=jax.ShapeDtypeStruct((n, D), F32),
        compiler_params=_cparams("parallel"),
        name="ln0",
    )(x, g.reshape(1, D), b.reshape(1, D))


def _inproj_kernel(x_ref, sc_ref, sh_ref, w_ref, o_ref, h_scr, *, mode):
    @pl.when(pl.program_id(1) == 0)
    def _():
        h = x_ref[...] * (1.0 + _mod_val(sc_ref, mode)) + _mod_val(sh_ref, mode)
        h_scr[...] = h.astype(BF16)

    o_ref[...] = jnp.dot(h_scr[...], w_ref[0], preferred_element_type=F32)


def _inproj(x, mods, w_in_p, l, mode, tm, tiles_per_batch):
    n = x.shape[0]
    tn = 1152
    return pl.pallas_call(
        functools.partial(_inproj_kernel, mode=mode),
        grid=(n // tm, NPROJ // tn),
        in_specs=[
            pl.BlockSpec((tm, D), lambda i, j: (i, 0)),
            _mod_spec(mode, l, tm, 1, tiles_per_batch),
            _mod_spec(mode, l, tm, 0, tiles_per_batch),
            pl.BlockSpec((1, D, tn), lambda i, j: (l, 0, j)),
        ],
        out_specs=pl.BlockSpec((tm, tn), lambda i, j: (i, j)),
        out_shape=jax.ShapeDtypeStruct((n, NPROJ), F32),
        scratch_shapes=[pltpu.VMEM((tm, D), BF16)],
        compiler_params=_cparams("parallel", "arbitrary"),
        name="inproj",
    )(x, mods, mods, w_in_p)


def _mixa_kernel(qkv_ref, ga_ref, sm_ref, cst_ref, s0_ref, cw_ref, alog_ref, dtb_ref, na_ref,
                 cbuf_ref, sbuf_ref, oa_ref, cso_ref, so_ref, xbuf, s_scr, *, C, tv, nc, bb):
    del cbuf_ref, sbuf_ref
    c = pl.program_id(1)
    row = lax.broadcasted_iota(I32, (C, 1), 0)
    ri = lax.broadcasted_iota(I32, (C, C), 0)
    ci = lax.broadcasted_iota(I32, (C, C), 1)
    causal = ri >= ci
    strict = ri > ci
    eye = (ri == ci).astype(F32)
    tril = causal.astype(F32)
    hist = SUBLANES - (CONV_W - 1)

    @pl.when(c == 0)
    def _():
        for bi in range(bb):
            xbuf[bi, 0:SUBLANES, :] = jnp.zeros((SUBLANES, CONV_DIM), F32)
            xbuf[bi, hist:SUBLANES, :] = cst_ref[0, bi]
            s_scr[bi] = s0_ref[0, bi]

    pre = []
    for bi in range(bb):
        xbuf[bi, SUBLANES:SUBLANES + C, :] = qkv_ref[bi]
        sm = sm_ref[bi]
        g_all = -jnp.exp(alog_ref[0]) * _softplus(sm + dtb_ref[0])
        beta_all = _sigmoid(sm)
        if tv < C:
            g_all = jnp.where(row < tv, g_all, 0.0)
            beta_all = jnp.where(row < tv, beta_all, 0.0)
        gcum_all = _dot(tril, g_all, hi=True)
        pre.append((gcum_all, gcum_all.T, jnp.exp(gcum_all), beta_all))

    n_nil = min(C, tv)
    n_double = max(int(math.ceil(math.log2(n_nil))) - 1, 0)
    units = [(bi, h) for bi in range(bb) for h in range(NH)]

    st = []
    for bi, h in units:
        gcum_all, gcum_t, eg_all, beta_all = pre[bi]
        conv = []
        for part in range(3):
            col = slice(part * D + h * HD, part * D + (h + 1) * HD)
            acc = None
            for j in range(CONV_W):
                term = xbuf[bi, hist + j:hist + j + C, col] * cw_ref[0, j:j + 1, col]
                acc = term if acc is None else acc + term
            conv.append(_silu(acc))
        q, k, v = conv
        q = q * lax.rsqrt(jnp.sum(q * q, -1, keepdims=True) + RMS_EPS) * (HD ** -0.5)
        k = k * lax.rsqrt(jnp.sum(k * k, -1, keepdims=True) + RMS_EPS)
        gc = gcum_all[:, h:h + 1]
        gr = gcum_t[h:h + 1, :]
        bt = beta_all[:, NH + h:NH + h + 1]
        eg = eg_all[:, h:h + 1]
        glast = gcum_all[tv - 1:tv, h:h + 1]
        dec = jnp.where(causal, jnp.exp(jnp.where(causal, gc - gr, 0.0)), 0.0)
        kb = k * bt
        st.append(dict(
            a=-jnp.where(strict, _dot_nt(kb, k) * dec, 0.0),
            rhs=jnp.concatenate([kb * eg, v * bt], axis=1),
            qk=jnp.where(causal, _dot_nt(q, k) * dec, 0.0),
            qe=q * eg,
            kg=k * jnp.exp(glast - gc),
            eglast=jnp.exp(glast),
        ))

    ps = [eye + u["a"] for u in st]
    qsp = [_split(u["a"]) for u in st]
    for _ in range(n_double):
        qs = [_dot3(s2, s2) for s2 in qsp]
        qsp = [_split(x) for x in qs]
        ps = [p + _dot3(_split(p), s2) for p, s2 in zip(ps, qsp)]
    sols = [_dot3(_split(p), _split(u["rhs"])) for p, u in zip(ps, st)]

    ss = [s_scr[bi, h] for bi, h in units]
    gates = [_silu(ga_ref[bi, :, h * HD:(h + 1) * HD]) for bi, h in units]
    wss = [_dot(jnp.concatenate([sol[:, :HD], u["qe"]], axis=0), s)
           for u, sol, s in zip(st, sols, ss)]
    vns = [sol[:, HD:] - ws[:C] for sol, ws in zip(sols, wss)]
    os_ = [ws[C:] + _dot(u["qk"], vn) for u, ws, vn in zip(st, wss, vns)]
    s_new = [u["eglast"] * s + _dot_tn(u["kg"], vn) for u, s, vn in zip(st, ss, vns)]
    ys = [o * lax.rsqrt(jnp.mean(o * o, -1, keepdims=True) + RMS_EPS) * na_ref[0] * gt
          for o, gt in zip(os_, gates)]
    for (bi, h), sn, y in zip(units, s_new, ys):
        s_scr[bi, h] = sn
        oa_ref[bi, :, h * HD:(h + 1) * HD] = y

    if nc > 1:
        for bi in range(bb):
            xbuf[bi, 0:SUBLANES, :] = xbuf[bi, C:C + SUBLANES, :]

    @pl.when(c == nc - 1)
    def _():
        for bi in range(bb):
            if nc > 1:
                cso_ref[0, bi] = xbuf[bi, hist:SUBLANES, :]
            else:
                cso_ref[0, bi] = xbuf[bi, SUBLANES + tv - (CONV_W - 1):SUBLANES + tv, :]
            so_ref[0, bi] = s_scr[bi]


def _mixer_a(proj3, conv_state, s0, conv_buf, s_buf, conv_w, a_log_p, dt_bias_p, norm_a, l, C, tv, bb):
    nb, t, _ = proj3.shape
    nc = t // C
    assert nc == 1 or tv == C
    return pl.pallas_call(
        functools.partial(_mixa_kernel, C=C, tv=tv, nc=nc, bb=bb),
        grid=(nb // bb, nc),
        in_specs=[
            pl.BlockSpec((bb, C, CONV_DIM), lambda b, c: (b, c, C_QKV // CONV_DIM)),
            pl.BlockSpec((bb, C, D), lambda b, c: (b, c, C_GA // D)),
            pl.BlockSpec((bb, C, LANES), lambda b, c: (b, c, C_SMALL // LANES)),
            pl.BlockSpec((1, bb, CONV_W - 1, CONV_DIM), lambda b, c: (l, b, 0, 0)),
            pl.BlockSpec((1, bb, NH, HD, HD), lambda b, c: (l, b, 0, 0, 0)),
            _layer_spec((CONV_W, CONV_DIM), l),
            _layer_spec((1, LANES), l),
            _layer_spec((1, LANES), l),
            _layer_spec((1, HD), l),
            pl.BlockSpec(memory_space=pl.ANY),
            pl.BlockSpec(memory_space=pl.ANY),
        ],
        out_specs=[
            pl.BlockSpec((bb, C, D), lambda b, c: (b, c, 0)),
            pl.BlockSpec((1, bb, CONV_W - 1, CONV_DIM), lambda b, c: (l, b, 0, 0)),
            pl.BlockSpec((1, bb, NH, HD, HD), lambda b, c: (l, b, 0, 0, 0)),
        ],
        out_shape=[
            jax.ShapeDtypeStruct((nb, t, D), F32),
            jax.ShapeDtypeStruct(conv_buf.shape, F32),
            jax.ShapeDtypeStruct(s_buf.shape, F32),
        ],
        input_output_aliases={9: 1, 10: 2},
        scratch_shapes=[pltpu.VMEM((bb, C + 2 * SUBLANES, CONV_DIM), F32),
                        pltpu.VMEM((bb, NH, HD, HD), F32)],
        compiler_params=_cparams("parallel", "arbitrary"),
        name="mixer_a",
    )(proj3, proj3, proj3, conv_state, s0, conv_w, a_log_p, dt_bias_p, norm_a, conv_buf, s_buf)


DIAG = SUBLANES


def _mixb_kernel(qb_ref, fb_ref, ib_ref, gb_ref, s0_ref, lbp_ref, nb_ref, sbuf_ref,
                 ob_ref, so_ref, st_scr, *, C, tv, nc, bb, layer):
    del sbuf_ref
    c = pl.program_id(1)
    row = lax.broadcasted_iota(I32, (C, 1), 0)
    ri = lax.broadcasted_iota(I32, (C, C), 0)
    ci = lax.broadcasted_iota(I32, (C, C), 1)
    tril = (ri >= ci).astype(F32)
    row8 = lax.broadcasted_iota(I32, (DIAG, 1), 0)

    @pl.when(c == 0)
    def _():
        for bi in range(bb):
            for h in range(NH):
                st_scr[bi, h] = s0_ref[0, bi, h].T

    lbp = lbp_ref[...]
    e = jnp.exp(lbp - jnp.max(lbp, axis=0, keepdims=True))
    psm = e / jnp.sum(e, axis=0, keepdims=True)
    lb = jnp.zeros((1, D), F32)
    for j in range(1, layer + 1):
        lb = lb + psm[j:j + 1, :]

    levels = []
    lv = DIAG
    while lv < C:
        levels.append(lv)
        lv *= 2

    units = []
    for bi in range(bb):
        z = fb_ref[bi]
        f = lb + (1.0 - lb) * _sigmoid(z)
        lf_all = jnp.log(jnp.maximum(f, F_MIN))
        kk_all = (1.0 - lb) * _sigmoid(-z)
        if tv < C:
            lf_all = jnp.where(row < tv, lf_all, 0.0)
            kk_all = jnp.where(row < tv, kk_all, 0.0)
        b_all = _dot(tril, lf_all, hi=True)

        for h in range(NH):
            cs = slice(h * HD, (h + 1) * HD)
            q = _silu(qb_ref[bi, :, cs])
            kk = kk_all[:, cs]
            iv = ib_ref[bi, :, cs]
            b = b_all[:, cs]
            blast = b[tv - 1:tv, :]

            amat = None
            for lv in levels:
                pieces = []
                for j in range(C // lv):
                    blk = b[j * lv:(j + 1) * lv, :]
                    if j % 2 == 1:
                        pieces.append(blk - b[j * lv:j * lv + 1, :])
                    else:
                        pieces.append(b[(j + 1) * lv:(j + 1) * lv + 1, :] - blk)
                ex = jnp.exp(jnp.concatenate(pieces, axis=0))
                lower = ((row // lv) % 2) == 1
                qs = jnp.where(lower, q * ex, 0.0)
                ks = jnp.where(lower, 0.0, kk * ex)
                blk_a = jnp.where((ri // (2 * lv)) == (ci // (2 * lv)), _dot_nt(qs, ks), 0.0)
                amat = blk_a if amat is None else amat + blk_a

            od = []
            for j in range(C // DIAG):
                rs = slice(j * DIAG, (j + 1) * DIAG)
                bd, qq, k8, i8 = b[rs, :], q[rs, :], kk[rs, :], iv[rs, :]
                acc = jnp.zeros((DIAG, HD), F32)
                for s in range(DIAG):
                    ok = row8 >= s
                    ex = jnp.exp(jnp.where(ok, bd - bd[s:s + 1, :], 0.0))
                    zz = jnp.where(ok, qq * ex * k8[s:s + 1, :], 0.0)
                    acc = acc + jnp.sum(zz, -1, keepdims=True) * i8[s:s + 1, :]
                od.append(acc)
            o = od[0] if len(od) == 1 else jnp.concatenate(od, axis=0)
            if amat is not None:
                o = o + _dot(amat, iv)

            units.append(dict(bi=bi, h=h, o=o, qe=q * jnp.exp(b), kg=kk * jnp.exp(blast - b), iv=iv,
                              eb=jnp.exp(blast), gate=_silu(gb_ref[bi, :, cs])))

    sts = [st_scr[u["bi"], u["h"]] for u in units]
    os_ = [u["o"] + _dot_nt(u["qe"], st) for u, st in zip(units, sts)]
    st_new = [st * u["eb"] + _dot_tn(u["iv"], u["kg"]) for u, st in zip(units, sts)]
    ys = [o * lax.rsqrt(jnp.mean(o * o, -1, keepdims=True) + RMS_EPS) * nb_ref[0] * u["gate"]
          for u, o in zip(units, os_)]
    for u, sn, y in zip(units, st_new, ys):
        st_scr[u["bi"], u["h"]] = sn
        ob_ref[u["bi"], :, u["h"] * HD:(u["h"] + 1) * HD] = y

    @pl.when(c == nc - 1)
    def _():
        for bi in range(bb):
            for h in range(NH):
                so_ref[0, bi, h] = st_scr[bi, h].T


def _mixer_b(proj3, s0, s_buf, lb_param, norm_b, l, C, tv, bb):
    nb, t, _ = proj3.shape
    nc = t // C
    depth = lb_param.shape[0]
    tok = lambda col: pl.BlockSpec((bb, C, D), lambda b, c: (b, c, col // D))
    return pl.pallas_call(
        functools.partial(_mixb_kernel, C=C, tv=tv, nc=nc, bb=bb, layer=l),
        grid=(nb // bb, nc),
        in_specs=[
            tok(C_QB), tok(C_FB), tok(C_IB), tok(C_GB),
            pl.BlockSpec((1, bb, NH, HD, HD), lambda b, c: (l, b, 0, 0, 0)),
            pl.BlockSpec((depth, D), lambda b, c: (0, 0)),
            _layer_spec((1, HD), l),
            pl.BlockSpec(memory_space=pl.ANY),
        ],
        out_specs=[
            pl.BlockSpec((bb, C, D), lambda b, c: (b, c, 0)),
            pl.BlockSpec((1, bb, NH, HD, HD), lambda b, c: (l, b, 0, 0, 0)),
        ],
        out_shape=[
            jax.ShapeDtypeStruct((nb, t, D), F32),
            jax.ShapeDtypeStruct(s_buf.shape, F32),
        ],
        input_output_aliases={7: 1},
        scratch_shapes=[pltpu.VMEM((bb, NH, HD, HD), F32)],
        compiler_params=_cparams("parallel", "arbitrary"),
        name="mixer_b",
    )(proj3, proj3, proj3, proj3, s0, lb_param, norm_b, s_buf)


def _post_kernel(oa_ref, ob_ref, gt_ref, x_ref, g1_ref, sc2_ref, sh2_ref, wa_ref, wb_ref, wo_ref,
                 lg_ref, lb_ref, wr_ref, br_ref, x1_ref, h2_ref, rw_ref, re_ref, *, mode, alpha):
    gates = _sigmoid(gt_ref[...])
    merged = gates[:, :D] * _dot(oa_ref[...], wa_ref[0]) + gates[:, D:] * _dot(ob_ref[...], wb_ref[0])
    y = _dot(merged, wo_ref[0])
    x1 = _layer_norm(alpha * x_ref[...] + (1.0 + _mod_val(g1_ref, mode)) * y, lg_ref[0], lb_ref[0])
    x1_ref[...] = x1
    h2 = x1 * (1.0 + _mod_val(sc2_ref, mode)) + _mod_val(sh2_ref, mode)
    h2_ref[...] = h2

    logits = _dot(h2, wr_ref[0], hi=True) + br_ref[0]
    lane = lax.broadcasted_iota(I32, logits.shape, 1)
    neg = jnp.float32(-jnp.inf)
    gl = jnp.where(lane < N_GROUPS, logits, neg)
    gmax = jnp.max(gl, -1, keepdims=True)
    grp = jnp.min(jnp.where(gl == gmax, lane, LANES), -1, keepdims=True)
    p_grp = 1.0 / jnp.sum(jnp.exp(gl - gmax), -1, keepdims=True)
    lo = N_GROUPS + grp * EPG
    el = jnp.where((lane >= lo) & (lane < lo + EPG), logits, neg)
    v1 = jnp.max(el, -1, keepdims=True)
    i1 = jnp.min(jnp.where(el == v1, lane, LANES), -1, keepdims=True)
    el2 = jnp.where(lane == i1, neg, el)
    v2 = jnp.max(el2, -1, keepdims=True)
    i2 = jnp.min(jnp.where(el2 == v2, lane, LANES), -1, keepdims=True)
    e2 = jnp.exp(v2 - v1)
    w1 = p_grp / (1.0 + e2)
    w2 = p_grp * e2 / (1.0 + e2)
    rw_ref[...] = jnp.where(lane == 0, w1, jnp.where(lane == 1, w2, 0.0))
    re_ref[...] = jnp.where(lane == 0, i1 - N_GROUPS, jnp.where(lane == 1, i2 - N_GROUPS, 0))


def _post_mixer(o_a, o_b, proj, x, mods, w_br_a, w_br_b, w_o, ln_g, ln_b, w_r, b_r, l, mode, tm,
                tiles_per_batch, alpha):
    n = x.shape[0]
    tok = pl.BlockSpec((tm, D), lambda i: (i, 0))
    lane_out = pl.BlockSpec((tm, LANES), lambda i: (i, 0))
    return pl.pallas_call(
        functools.partial(_post_kernel, mode=mode, alpha=alpha),
        grid=(n // tm,),
        in_specs=[
            tok, tok,
            pl.BlockSpec((tm, 2 * D), lambda i: (i, C_MERGE // (2 * D))),
            tok,
            _mod_spec(mode, l, tm, 2, tiles_per_batch),
            _mod_spec(mode, l, tm, 4, tiles_per_batch),
            _mod_spec(mode, l, tm, 3, tiles_per_batch),
            _layer_spec((D, D), l), _layer_spec((D, D), l), _layer_spec((D, D), l),
            _layer_spec((1, D), l), _layer_spec((1, D), l),
            _layer_spec((D, LANES), l), _layer_spec((1, LANES), l),
        ],
        out_specs=[tok, tok, lane_out, lane_out],
        out_shape=[
            jax.ShapeDtypeStruct((n, D), F32),
            jax.ShapeDtypeStruct((n, D), F32),
            jax.ShapeDtypeStruct((n, LANES), F32),
            jax.ShapeDtypeStruct((n, LANES), I32),
        ],
        compiler_params=_cparams("parallel"),
        name="post_mixer",
    )(o_a, o_b, proj, x, mods, mods, mods, w_br_a, w_br_b, w_o, ln_g, ln_b, w_r, b_r)


TM_EXP = 256


def _dispatch_kernel(d0_ref, d1_ref, h_ref, xs_in, xs_out, sem, *, tm):
    del xs_in

    def copies(r):
        src = h_ref.at[pl.ds(r, 1), :]
        return (pltpu.make_async_copy(src, xs_out.at[pl.ds(d0_ref[r], 1), :], sem),
                pltpu.make_async_copy(src, xs_out.at[pl.ds(d1_ref[r], 1), :], sem))

    def start(r, carry):
        a, b = copies(r)
        a.start()
        b.start()
        return carry

    def wait(r, carry):
        a, b = copies(r)
        a.wait()
        b.wait()
        return carry

    lax.fori_loop(0, tm, start, 0, unroll=4)
    lax.fori_loop(0, tm, wait, 0, unroll=4)


def _dispatch(h2, d0, d1, xs_buf, tm):
    n = h2.shape[0]
    return pl.pallas_call(
        functools.partial(_dispatch_kernel, tm=tm),
        grid=(n // tm,),
        in_specs=[
            pl.BlockSpec((tm,), lambda i: (i,), memory_space=pltpu.SMEM),
            pl.BlockSpec((tm,), lambda i: (i,), memory_space=pltpu.SMEM),
            pl.BlockSpec((tm, D), lambda i: (i, 0)),
            pl.BlockSpec(memory_space=pl.ANY),
        ],
        out_specs=pl.BlockSpec(memory_space=pl.ANY),
        out_shape=jax.ShapeDtypeStruct(xs_buf.shape, F32),
        input_output_aliases={3: 0},
        scratch_shapes=[pltpu.SemaphoreType.DMA],
        compiler_params=_cparams("arbitrary"),
        name="dispatch",
    )(d0, d1, h2, xs_buf)


def _expert_kernel(be_ref, nblk_ref, xs_ref, wg_ref, wu_ref, wd_ref, ys_ref):
    i = pl.program_id(0)

    @pl.when(i < nblk_ref[0])
    def _():
        x = xs_ref[...]
        hid = _silu(_dot(x, wg_ref[0, 0])) * _dot(x, wu_ref[0, 0])
        ys_ref[...] = _dot(hid, wd_ref[0, 0])

    @pl.when(i >= nblk_ref[0])
    def _():
        ys_ref[...] = jnp.zeros(ys_ref.shape, F32)


def _experts(xs, block_e, n_used, w_gate, w_up, w_down, l):
    n_rows = xs.shape[0]
    n_blocks = n_rows // TM_EXP
    grid_spec = pltpu.PrefetchScalarGridSpec(
        num_scalar_prefetch=2,
        grid=(n_blocks,),
        in_specs=[
            pl.BlockSpec((TM_EXP, D), lambda i, be, nb: (i, 0)),
            pl.BlockSpec((1, 1, D, D_EXP), lambda i, be, nb: (l, be[i], 0, 0)),
            pl.BlockSpec((1, 1, D, D_EXP), lambda i, be, nb: (l, be[i], 0, 0)),
            pl.BlockSpec((1, 1, D_EXP, D), lambda i, be, nb: (l, be[i], 0, 0)),
        ],
        out_specs=pl.BlockSpec((TM_EXP, D), lambda i, be, nb: (i, 0)),
    )
    return pl.pallas_call(
        _expert_kernel,
        grid_spec=grid_spec,
        out_shape=jax.ShapeDtypeStruct((n_rows, D), F32),
        compiler_params=_cparams("arbitrary"),
        name="experts",
    )(block_e, n_used, xs, w_gate, w_up, w_down)


def _combine_kernel(d0_ref, d1_ref, ys_hbm, rw_ref, x_ref, g2_ref, lg_ref, lb_ref, o_ref,
                    y0, y1, sem, *, mode, alpha, tm):
    def copies(r):
        return (pltpu.make_async_copy(ys_hbm.at[pl.ds(d0_ref[r], 1), :], y0.at[pl.ds(r, 1), :], sem),
                pltpu.make_async_copy(ys_hbm.at[pl.ds(d1_ref[r], 1), :], y1.at[pl.ds(r, 1), :], sem))

    def start(r, carry):
        a, b = copies(r)
        a.start()
        b.start()
        return carry

    def wait(r, carry):
        a, b = copies(r)
        a.wait()
        b.wait()
        return carry

    lax.fori_loop(0, tm, start, 0, unroll=4)
    lax.fori_loop(0, tm, wait, 0, unroll=4)
    rw = rw_ref[...]
    y = rw[:, 0:1] * y0[...] + rw[:, 1:2] * y1[...]
    o_ref[...] = _layer_norm(alpha * x_ref[...] + (1.0 + _mod_val(g2_ref, mode)) * y,
                             lg_ref[0], lb_ref[0])


def _combine(ys, d0, d1, rw, x1, mods, ln_g, ln_b, l, mode, tm, tiles_per_batch, alpha):
    n = x1.shape[0]
    return pl.pallas_call(
        functools.partial(_combine_kernel, mode=mode, alpha=alpha, tm=tm),
        grid=(n // tm,),
        in_specs=[
            pl.BlockSpec((tm,), lambda i: (i,), memory_space=pltpu.SMEM),
            pl.BlockSpec((tm,), lambda i: (i,), memory_space=pltpu.SMEM),
            pl.BlockSpec(memory_space=pl.ANY),
            pl.BlockSpec((tm, LANES), lambda i: (i, 0)),
            pl.BlockSpec((tm, D), lambda i: (i, 0)),
            _mod_spec(mode, l, tm, 5, tiles_per_batch),
            _layer_spec((1, D), l),
            _layer_spec((1, D), l),
        ],
        out_specs=pl.BlockSpec((tm, D), lambda i: (i, 0)),
        out_shape=jax.ShapeDtypeStruct((n, D), F32),
        scratch_shapes=[pltpu.VMEM((tm, D), F32), pltpu.VMEM((tm, D), F32), pltpu.SemaphoreType.DMA],
        compiler_params=_cparams("arbitrary"),
        name="combine",
    )(d0, d1, ys, rw, x1, mods, ln_g, ln_b)


def _route_meta(eid):
    n = eid.shape[0]
    flat = eid.reshape(-1)
    n_slots = flat.shape[0]
    onehot = (flat[:, None] == jnp.arange(N_EXP, dtype=I32)[None, :]).astype(I32)
    csum = jnp.cumsum(onehot, axis=0)
    counts = csum[-1]
    rank = jnp.take_along_axis(csum - onehot, flat[:, None], axis=1)[:, 0]
    padded = (counts + TM_EXP - 1) // TM_EXP * TM_EXP
    pad_end = jnp.cumsum(padded)
    pad_start = pad_end - padded
    dest = pad_start[flat] + rank
    n_blocks = (n_slots + N_EXP * (TM_EXP - 1)) // TM_EXP
    block_start = jnp.arange(n_blocks, dtype=I32) * TM_EXP
    block_e = jnp.minimum(jnp.sum(pad_end[None, :] <= block_start[:, None], axis=1), N_EXP - 1).astype(I32)
    n_used = (pad_end[-1:] // TM_EXP).astype(I32)
    dest = dest.reshape(n, 2)
    return n_blocks * TM_EXP, block_e, n_used, dest[:, 0], dest[:, 1]


def _pack_w_in(w_in):
    off_g_a = CONV_DIM
    off_decay = off_g_a + D
    off_q_b = off_decay + 2 * NH
    off_merge = off_q_b + 4 * D
    depth = w_in.shape[0]
    small = w_in[:, :, off_decay:off_q_b]
    pad = jnp.zeros((depth, D, LANES - 2 * NH), w_in.dtype)
    packed = jnp.concatenate([w_in[:, :, :off_decay], w_in[:, :, off_q_b:off_merge + 2 * D], small, pad], axis=2)
    return packed.astype(BF16)


def _pad_lanes(v):
    depth, k = v.shape
    return jnp.concatenate([v, jnp.zeros((depth, LANES - k), v.dtype)], axis=1).reshape(depth, 1, LANES)


def kernel(x_prompt, x_sample, state_conv, state_delta, state_hgrn, c_prompt, c_sample, ln0_g, ln0_b, w_ada, b_ada, w_in, conv_w, a_log, dt_bias, norm_a, lb_param, norm_b, w_br_a, w_br_b, w_o, ln1_g, ln1_b, w_grp, b_grp, w_rt, b_rt, w_gate_e, w_up_e, w_down_e, ln2_g, ln2_b):
    depth = w_in.shape[0]
    alpha = (2 * depth) ** 0.25
    bp, tp, _ = x_prompt.shape
    bs, ts, _ = x_sample.shape
    n_p = bp * tp
    tsp = SUBLANES
    assert ts <= tsp
    n_s = bs * tsp

    w_in_p = _pack_w_in(w_in)
    w_a16, w_b16, w_o16 = w_br_a.astype(BF16), w_br_b.astype(BF16), w_o.astype(BF16)
    w_r = jnp.concatenate([w_grp, w_rt, jnp.zeros((depth, D, LANES - N_GROUPS - N_EXP), F32)], axis=2)
    b_r = jnp.concatenate([b_grp, b_rt, jnp.zeros((depth, LANES - N_GROUPS - N_EXP), F32)], axis=1)
    b_r = b_r.reshape(depth, 1, LANES)
    a_log_p, dt_bias_p = _pad_lanes(a_log), _pad_lanes(dt_bias)
    norm_a3, norm_b3 = norm_a.reshape(depth, 1, HD), norm_b.reshape(depth, 1, HD)
    ln1_g3, ln1_b3 = ln1_g.reshape(depth, 1, D), ln1_b.reshape(depth, 1, D)
    ln2_g3, ln2_b3 = ln2_g.reshape(depth, 1, D), ln2_b.reshape(depth, 1, D)

    mods = _modulation(jnp.concatenate([c_prompt, c_sample], axis=0), w_ada, b_ada)
    mods_p = mods[:, :bp].reshape(depth, bp, 1, 6 * D)
    mods_s = jnp.repeat(mods[:, bp:], tsp, axis=1)

    xs_pad = jnp.concatenate([x_sample, jnp.zeros((bs, tsp - ts, D), x_sample.dtype)], axis=1)
    cp = math.gcd(tp, 128)
    tm_p = math.gcd(tp, 512)
    tm_s = math.gcd(n_s, 512)
    bb_s = math.gcd(bs, 4)
    state_shape = lambda nb: (depth, nb, NH, HD, HD)
    conv_shape = lambda nb: (depth, nb, CONV_W - 1, CONV_DIM)
    groups = [
        dict(mode="row", n=n_p, nb=bp, t=tp, C=cp, tv=cp, tm=tm_p, tpb=tp // tm_p, bb=1, mods=mods_p,
             conv=jnp.zeros(conv_shape(bp), F32), delta=jnp.zeros(state_shape(bp), F32),
             hgrn=jnp.zeros(state_shape(bp), F32)),
        dict(mode="tok", n=n_s, nb=bs, t=tsp, C=tsp, tv=ts, tm=tm_s, tpb=1, bb=bb_s, mods=mods_s,
             conv=state_conv, delta=state_delta, hgrn=state_hgrn),
    ]
    for g in groups:
        g["conv_o"] = jnp.zeros(conv_shape(g["nb"]), F32)
        g["delta_o"] = jnp.zeros(state_shape(g["nb"]), F32)
        g["hgrn_o"] = jnp.zeros(state_shape(g["nb"]), F32)
    xs = [_ln0(x_prompt.reshape(n_p, D), ln0_g, ln0_b, tm_p),
          _ln0(xs_pad.reshape(n_s, D), ln0_g, ln0_b, tm_s)]

    for l in range(depth):
        x1s, h2s, rws, res = [], [], [], []
        for gi, g in enumerate(groups):
            tm_in = math.gcd(g["t"] if g["mode"] == "row" else g["n"], 1024)
            proj = _inproj(xs[gi], g["mods"], w_in_p, l, g["mode"], tm_in, g["t"] // tm_in)
            proj3 = proj.reshape(g["nb"], g["t"], NPROJ)
            o_a, g["conv_o"], g["delta_o"] = _mixer_a(
                proj3, g["conv"], g["delta"], g["conv_o"], g["delta_o"], conv_w, a_log_p, dt_bias_p,
                norm_a3, l, g["C"], g["tv"], g["bb"])
            o_b, g["hgrn_o"] = _mixer_b(proj3, g["hgrn"], g["hgrn_o"], lb_param, norm_b3, l, g["C"],
                                        g["tv"], g["bb"])
            halve = 1 if g["mode"] == "row" else 2
            x1, h2, rw, re = _post_mixer(o_a.reshape(g["n"], D), o_b.reshape(g["n"], D), proj, xs[gi],
                                         g["mods"], w_a16, w_b16, w_o16, ln1_g3, ln1_b3, w_r, b_r, l,
                                         g["mode"], g["tm"] // halve, g["tpb"] * halve, alpha)
            x1s.append(x1)
            h2s.append(h2)
            rws.append(rw)
            res.append(re)
        eid = jnp.concatenate([r[:, :2] for r in res], axis=0)
        n_rows, block_e, n_used, d0, d1 = _route_meta(eid)
        xs_sorted = jnp.zeros((n_rows, D), F32)
        off = 0
        for gi, g in enumerate(groups):
            sl = slice(off, off + g["n"])
            xs_sorted = _dispatch(h2s[gi], d0[sl], d1[sl], xs_sorted, g["tm"])
            off += g["n"]
        ys = _experts(xs_sorted, block_e, n_used, w_gate_e, w_up_e, w_down_e, l)
        off = 0
        for gi, g in enumerate(groups):
            sl = slice(off, off + g["n"])
            xs[gi] = _combine(ys, d0[sl], d1[sl], rws[gi], x1s[gi], g["mods"], ln2_g3, ln2_b3, l,
                              g["mode"], g["tm"], g["tpb"], alpha)
            off += g["n"]

    y_prompt = xs[0].reshape(bp, tp, D)
    y_sample = xs[1].reshape(bs, tsp, D)[:, :ts]
    gp, gs = groups
    return (y_prompt, y_sample, gp["conv_o"], gp["delta_o"], gp["hgrn_o"],
            gs["conv_o"], gs["delta_o"], gs["hgrn_o"])
```

```python
import functools
import math

import jax
import jax.numpy as jnp
from jax import lax
from jax.experimental import pallas as pl
from jax.experimental.pallas import tpu as pltpu

F32 = jnp.float32
BF16 = jnp.bfloat16
I32 = jnp.int32

D = 1024
NH = 8
HD = 128
CONV_W = 4
CONV_DIM = 3 * D
N_GROUPS = 4
EPG = 8
N_EXP = N_GROUPS * EPG
D_EXP = 512
F_MIN = 1e-30
LN_EPS = 1e-5
RMS_EPS = 1e-6

C_QKV = 0
C_GA = 3 * D
C_QB = 4 * D
C_FB = 5 * D
C_IB = 6 * D
C_GB = 7 * D
C_MERGE = 8 * D
NPROJ = 10 * D

LANES = 128
SUBLANES = 8
VMEM_LIMIT = 48 * 1024 * 1024

HI = lax.Precision.HIGHEST


def _cparams(*sem):
    return pltpu.CompilerParams(dimension_semantics=sem, vmem_limit_bytes=VMEM_LIMIT)


def _dot(a, b, hi=False):
    if hi:
        return jnp.dot(a, b, preferred_element_type=F32, precision=HI)
    return jnp.dot(a.astype(BF16), b.astype(BF16), preferred_element_type=F32)


def _dot_nt(a, b):
    return lax.dot_general(a.astype(BF16), b.astype(BF16), (((1,), (1,)), ((), ())),
                           preferred_element_type=F32)


def _dot_tn(a, b):
    return lax.dot_general(a.astype(BF16), b.astype(BF16), (((0,), (0,)), ((), ())),
                           preferred_element_type=F32)


def _split(a):
    hi = a.astype(BF16)
    return hi, (a - hi.astype(F32)).astype(BF16)


def _dot3(a, b):
    mm = lambda x, y: jnp.dot(x, y, preferred_element_type=F32)
    return mm(a[0], b[0]) + (mm(a[0], b[1]) + mm(a[1], b[0]))


def _sigmoid(x):
    return 1.0 / (1.0 + jnp.exp(-x))


def _silu(x):
    return x * _sigmoid(x)


def _softplus(x):
    return jnp.maximum(x, 0.0) + jnp.log(1.0 + jnp.exp(-jnp.abs(x)))


def _layer_norm(x, g, b):
    mu = jnp.mean(x, -1, keepdims=True)
    xc = x - mu
    var = jnp.mean(xc * xc, -1, keepdims=True)
    return xc * lax.rsqrt(var + LN_EPS) * g + b


def _layer_spec(shape, l):
    zeros = (0,) * len(shape)
    return pl.BlockSpec((1,) + tuple(shape), lambda *_: (l,) + zeros)


def _mod_kernel(c_ref, w_ref, b_ref, o_ref):
    ca = _silu(c_ref[...])
    o_ref[0] = _dot(ca, w_ref[0]) + b_ref[0]


def _modulation(c_all, w_ada, b_ada):
    depth = w_ada.shape[0]
    nb = c_all.shape[0]
    tn = 1536
    return pl.pallas_call(
        _mod_kernel,
        grid=(depth, 6 * D // tn),
        in_specs=[
            pl.BlockSpec((nb, D), lambda l, j: (0, 0)),
            pl.BlockSpec((1, D, tn), lambda l, j: (l, 0, j)),
            pl.BlockSpec((1, 1, tn), lambda l, j: (l, 0, j)),
        ],
        out_specs=pl.BlockSpec((1, nb, tn), lambda l, j: (l, 0, j)),
        out_shape=jax.ShapeDtypeStruct((depth, nb, 6 * D), F32),
        compiler_params=_cparams("parallel", "parallel"),
        name="adaln_mod",
    )(c_all, w_ada, b_ada.reshape(depth, 1, 6 * D))


def _mod_spec(mode, l, tm, k, tiles_per_batch):
    if mode == "row":
        return pl.BlockSpec((1, 1, 1, D), lambda i, *_: (l, i // tiles_per_batch, 0, k))
    return pl.BlockSpec((1, tm, D), lambda i, *_: (l, i, k))


def _mod_val(ref, mode):
    return ref[0, 0] if mode == "row" else ref[0]


def _ln0_kernel(x_ref, g_ref, b_ref, o_ref):
    o_ref[...] = _layer_norm(x_ref[...], g_ref[...], b_ref[...])


def _ln0(x, g, b, tm):
    n = x.shape[0]
    return pl.pallas_call(
        _ln0_kernel,
        grid=(n // tm,),
        in_specs=[pl.BlockSpec((tm, D), lambda i: (i, 0)),
                  pl.BlockSpec((1, D), lambda i: (0, 0)),
                  pl.BlockSpec((1, D), lambda i: (0, 0))],
        out_specs=pl.BlockSpec((tm, D), lambda i: (i, 0)),
        out_shape=jax.ShapeDtypeStruct((n, D), F32),
        compiler_params=_cparams("parallel"),
        name="ln0",
    )(x, g.reshape(1, D), b.reshape(1, D))


def _inproj_kernel(x_ref, sc_ref, sh_ref, w_ref, ws_ref, o_ref, os_ref, h_scr, *, mode):
    @pl.when(pl.program_id(1) == 0)
    def _():
        h = x_ref[...] * (1.0 + _mod_val(sc_ref, mode)) + _mod_val(sh_ref, mode)
        h_scr[...] = h.astype(BF16)
        os_ref[...] = jnp.dot(h_scr[...], ws_ref[0], preferred_element_type=F32)

    o_ref[...] = jnp.dot(h_scr[...], w_ref[0], preferred_element_type=F32)


def _inproj(x, mods, w_main, w_small, l, mode, tm, tiles_per_batch):
    n = x.shape[0]
    tn = 1280
    return pl.pallas_call(
        functools.partial(_inproj_kernel, mode=mode),
        grid=(n // tm, NPROJ // tn),
        in_specs=[
            pl.BlockSpec((tm, D), lambda i, j: (i, 0)),
            _mod_spec(mode, l, tm, 1, tiles_per_batch),
            _mod_spec(mode, l, tm, 0, tiles_per_batch),
            pl.BlockSpec((1, D, tn), lambda i, j: (l, 0, j)),
            pl.BlockSpec((1, D, LANES), lambda i, j: (l, 0, 0)),
        ],
        out_specs=[pl.BlockSpec((tm, tn), lambda i, j: (i, j)),
                   pl.BlockSpec((tm, LANES), lambda i, j: (i, 0))],
        out_shape=[jax.ShapeDtypeStruct((n, NPROJ), F32),
                   jax.ShapeDtypeStruct((n, LANES), F32)],
        scratch_shapes=[pltpu.VMEM((tm, D), BF16)],
        compiler_params=_cparams("parallel", "arbitrary"),
        name="inproj",
    )(x, mods, mods, w_main, w_small)


def _mixa_kernel(qkv_ref, ga_ref, sm_ref, cst_ref, s0_ref, cw_ref, alog_ref, dtb_ref, na_ref,
                 cbuf_ref, sbuf_ref, oa_ref, cso_ref, so_ref, xbuf, s_scr, *, C, tv, nc, bb):
    del cbuf_ref, sbuf_ref
    c = pl.program_id(1)
    row = lax.broadcasted_iota(I32, (C, 1), 0)
    ri = lax.broadcasted_iota(I32, (C, C), 0)
    ci = lax.broadcasted_iota(I32, (C, C), 1)
    causal = ri >= ci
    strict = ri > ci
    eye = (ri == ci).astype(F32)
    tril = causal.astype(F32)
    hist = SUBLANES - (CONV_W - 1)

    @pl.when(c == 0)
    def _():
        for bi in range(bb):
            xbuf[bi, 0:SUBLANES, :] = jnp.zeros((SUBLANES, CONV_DIM), F32)
            xbuf[bi, hist:SUBLANES, :] = cst_ref[0, bi]
            s_scr[bi] = s0_ref[0, bi]

    pre = []
    for bi in range(bb):
        xbuf[bi, SUBLANES:SUBLANES + C, :] = qkv_ref[bi]
        sm = sm_ref[bi]
        g_all = -jnp.exp(alog_ref[0]) * _softplus(sm + dtb_ref[0])
        beta_all = _sigmoid(sm)
        if tv < C:
            g_all = jnp.where(row < tv, g_all, 0.0)
            beta_all = jnp.where(row < tv, beta_all, 0.0)
        gcum_all = _dot(tril, g_all, hi=True)
        pre.append((gcum_all, gcum_all.T, jnp.exp(gcum_all), beta_all))

    n_nil = min(C, tv)
    n_double = max(int(math.ceil(math.log2(n_nil))) - 1, 0)
    units = [(bi, h) for bi in range(bb) for h in range(NH)]

    st = []
    for bi, h in units:
        gcum_all, gcum_t, eg_all, beta_all = pre[bi]
        conv = []
        for part in range(3):
            col = slice(part * D + h * HD, part * D + (h + 1) * HD)
            acc = None
            for j in range(CONV_W):
                term = xbuf[bi, hist + j:hist + j + C, col] * cw_ref[0, j:j + 1, col]
                acc = term if acc is None else acc + term
            conv.append(_silu(acc))
        q, k, v = conv
        q = q * lax.rsqrt(jnp.sum(q * q, -1, keepdims=True) + RMS_EPS) * (HD ** -0.5)
        k = k * lax.rsqrt(jnp.sum(k * k, -1, keepdims=True) + RMS_EPS)
        gc = gcum_all[:, h:h + 1]
        gr = gcum_t[h:h + 1, :]
        bt = beta_all[:, NH + h:NH + h + 1]
        eg = eg_all[:, h:h + 1]
        glast = gcum_all[tv - 1:tv, h:h + 1]
        dec = jnp.where(causal, jnp.exp(jnp.where(causal, gc - gr, 0.0)), 0.0)
        kb = k * bt
        kk = _dot_nt(jnp.concatenate([kb, q], axis=0), k)
        st.append(dict(
            a=-jnp.where(strict, kk[:C] * dec, 0.0),
            rhs=jnp.concatenate([kb * eg, v * bt], axis=1),
            qk=jnp.where(causal, kk[C:] * dec, 0.0),
            qe=q * eg,
            kg=k * jnp.exp(glast - gc),
            eglast=jnp.exp(glast),
        ))

    ps = [eye + u["a"] for u in st]
    qsp = [_split(u["a"]) for u in st]
    for _ in range(n_double):
        qs = [_dot3(s2, s2) for s2 in qsp]
        qsp = [_split(x) for x in qs]
        ps = [p + _dot3(_split(p), s2) for p, s2 in zip(ps, qsp)]
    sols = [_dot3(_split(p), _split(u["rhs"])) for p, u in zip(ps, st)]

    ss = [s_scr[bi, h] for bi, h in units]
    gates = [_silu(ga_ref[bi, :, h * HD:(h + 1) * HD]) for bi, h in units]
    wss = [_dot(jnp.concatenate([sol[:, :HD], u["qe"]], axis=0), s)
           for u, sol, s in zip(st, sols, ss)]
    vns = [sol[:, HD:] - ws[:C] for sol, ws in zip(sols, wss)]
    os_ = [ws[C:] + _dot(u["qk"], vn) for u, ws, vn in zip(st, wss, vns)]
    s_new = [u["eglast"] * s + _dot_tn(u["kg"], vn) for u, s, vn in zip(st, ss, vns)]
    ys = [o * lax.rsqrt(jnp.mean(o * o, -1, keepdims=True) + RMS_EPS) * na_ref[0] * gt
          for o, gt in zip(os_, gates)]
    for (bi, h), sn, y in zip(units, s_new, ys):
        s_scr[bi, h] = sn
        oa_ref[bi, :, h * HD:(h + 1) * HD] = y

    if nc > 1:
        for bi in range(bb):
            xbuf[bi, 0:SUBLANES, :] = xbuf[bi, C:C + SUBLANES, :]

    @pl.when(c == nc - 1)
    def _():
        for bi in range(bb):
            if nc > 1:
                cso_ref[0, bi] = xbuf[bi, hist:SUBLANES, :]
            else:
                cso_ref[0, bi] = xbuf[bi, SUBLANES + tv - (CONV_W - 1):SUBLANES + tv, :]
            so_ref[0, bi] = s_scr[bi]


def _mixer_a(proj3, small3, conv_state, s0, conv_buf, s_buf, conv_w, a_log_p, dt_bias_p, norm_a, l, C, tv,
             bb):
    nb, t, _ = proj3.shape
    nc = t // C
    assert nc == 1 or tv == C
    return pl.pallas_call(
        functools.partial(_mixa_kernel, C=C, tv=tv, nc=nc, bb=bb),
        grid=(nb // bb, nc),
        in_specs=[
            pl.BlockSpec((bb, C, CONV_DIM), lambda b, c: (b, c, C_QKV // CONV_DIM)),
            pl.BlockSpec((bb, C, D), lambda b, c: (b, c, C_GA // D)),
            pl.BlockSpec((bb, C, LANES), lambda b, c: (b, c, 0)),
            pl.BlockSpec((1, bb, CONV_W - 1, CONV_DIM), lambda b, c: (l, b, 0, 0)),
            pl.BlockSpec((1, bb, NH, HD, HD), lambda b, c: (l, b, 0, 0, 0)),
            _layer_spec((CONV_W, CONV_DIM), l),
            _layer_spec((1, LANES), l),
            _layer_spec((1, LANES), l),
            _layer_spec((1, HD), l),
            pl.BlockSpec(memory_space=pl.ANY),
            pl.BlockSpec(memory_space=pl.ANY),
        ],
        out_specs=[
            pl.BlockSpec((bb, C, D), lambda b, c: (b, c, 0)),
            pl.BlockSpec((1, bb, CONV_W - 1, CONV_DIM), lambda b, c: (l, b, 0, 0)),
            pl.BlockSpec((1, bb, NH, HD, HD), lambda b, c: (l, b, 0, 0, 0)),
        ],
        out_shape=[
            jax.ShapeDtypeStruct((nb, t, D), F32),
            jax.ShapeDtypeStruct(conv_buf.shape, F32),
            jax.ShapeDtypeStruct(s_buf.shape, F32),
        ],
        input_output_aliases={9: 1, 10: 2},
        scratch_shapes=[pltpu.VMEM((bb, C + 2 * SUBLANES, CONV_DIM), F32),
                        pltpu.VMEM((bb, NH, HD, HD), F32)],
        compiler_params=_cparams("parallel", "arbitrary"),
        name="mixer_a",
    )(proj3, proj3, small3, conv_state, s0, conv_w, a_log_p, dt_bias_p, norm_a, conv_buf, s_buf)


DIAG = SUBLANES


def _mixb_kernel(qb_ref, fb_ref, ib_ref, gb_ref, s0_ref, lbp_ref, nb_ref, sbuf_ref,
                 ob_ref, so_ref, st_scr, *, C, tv, nc, bb, layer):
    del sbuf_ref
    c = pl.program_id(1)
    row = lax.broadcasted_iota(I32, (C, 1), 0)
    ri = lax.broadcasted_iota(I32, (C, C), 0)
    ci = lax.broadcasted_iota(I32, (C, C), 1)
    tril = (ri >= ci).astype(F32)

    @pl.when(c == 0)
    def _():
        for bi in range(bb):
            for h in range(NH):
                st_scr[bi, h] = s0_ref[0, bi, h].T

    lbp = lbp_ref[...]
    e = jnp.exp(lbp - jnp.max(lbp, axis=0, keepdims=True))
    psm = e / jnp.sum(e, axis=0, keepdims=True)
    lb = jnp.zeros((1, D), F32)
    for j in range(1, layer + 1):
        lb = lb + psm[j:j + 1, :]

    row8 = lax.broadcasted_iota(I32, (DIAG, 1), 0)
    levels = []
    lv = DIAG
    while lv < C:
        levels.append(lv)
        lv *= 2

    units = []
    for bi in range(bb):
        z = fb_ref[bi]
        f = lb + (1.0 - lb) * _sigmoid(z)
        lf_all = jnp.log(jnp.maximum(f, F_MIN))
        kk_all = (1.0 - lb) * _sigmoid(-z)
        if tv < C:
            lf_all = jnp.where(row < tv, lf_all, 0.0)
            kk_all = jnp.where(row < tv, kk_all, 0.0)
        b_all = _dot(tril, lf_all, hi=True)

        for h in range(NH):
            cs = slice(h * HD, (h + 1) * HD)
            q = _silu(qb_ref[bi, :, cs])
            kk = kk_all[:, cs]
            iv = ib_ref[bi, :, cs]
            b = b_all[:, cs]
            blast = b[tv - 1:tv, :]

            amat = None
            for lv in levels:
                pieces = []
                for j in range(C // lv):
                    blk = b[j * lv:(j + 1) * lv, :]
                    if j % 2 == 1:
                        pieces.append(blk - b[j * lv:j * lv + 1, :])
                    else:
                        pieces.append(b[(j + 1) * lv:(j + 1) * lv + 1, :] - blk)
                ex = jnp.exp(jnp.concatenate(pieces, axis=0))
                lower = ((row // lv) % 2) == 1
                qs = jnp.where(lower, q * ex, 0.0)
                ks = jnp.where(lower, 0.0, kk * ex)
                blk_a = jnp.where((ri // (2 * lv)) == (ci // (2 * lv)), _dot_nt(qs, ks), 0.0)
                amat = blk_a if amat is None else amat + blk_a

            od = []
            for j in range(C // DIAG):
                rs = slice(j * DIAG, (j + 1) * DIAG)
                bd, qq, k8, i8 = b[rs, :], q[rs, :], kk[rs, :], iv[rs, :]
                acc = jnp.zeros((DIAG, HD), F32)
                for s in range(DIAG):
                    ex = jnp.exp(jnp.where(row8 >= s, bd - bd[s:s + 1, :], -jnp.inf))
                    zz = qq * ex * k8[s:s + 1, :]
                    acc = acc + jnp.sum(zz, -1, keepdims=True) * i8[s:s + 1, :]
                od.append(acc)
            o = od[0] if len(od) == 1 else jnp.concatenate(od, axis=0)
            if amat is not None:
                o = o + _dot(amat, iv)

            units.append(dict(bi=bi, h=h, o=o, qe=q * jnp.exp(b), kg=kk * jnp.exp(blast - b), iv=iv,
                              eb=jnp.exp(blast), gate=_silu(gb_ref[bi, :, cs])))

    sts = [st_scr[u["bi"], u["h"]] for u in units]
    os_ = [u["o"] + _dot_nt(u["qe"], st) for u, st in zip(units, sts)]
    st_new = [st * u["eb"] + _dot_tn(u["iv"], u["kg"]) for u, st in zip(units, sts)]
    ys = [o * lax.rsqrt(jnp.mean(o * o, -1, keepdims=True) + RMS_EPS) * nb_ref[0] * u["gate"]
          for u, o in zip(units, os_)]
    for u, sn, y in zip(units, st_new, ys):
        st_scr[u["bi"], u["h"]] = sn
        ob_ref[u["bi"], :, u["h"] * HD:(u["h"] + 1) * HD] = y

    @pl.when(c == nc - 1)
    def _():
        for bi in range(bb):
            for h in range(NH):
                so_ref[0, bi, h] = st_scr[bi, h].T


def _mixer_b(proj3, s0, s_buf, lb_param, norm_b, l, C, tv, bb):
    nb, t, _ = proj3.shape
    nc = t // C
    depth = lb_param.shape[0]
    tok = lambda col: pl.BlockSpec((bb, C, D), lambda b, c: (b, c, col // D))
    return pl.pallas_call(
        functools.partial(_mixb_kernel, C=C, tv=tv, nc=nc, bb=bb, layer=l),
        grid=(nb // bb, nc),
        in_specs=[
            tok(C_QB), tok(C_FB), tok(C_IB), tok(C_GB),
            pl.BlockSpec((1, bb, NH, HD, HD), lambda b, c: (l, b, 0, 0, 0)),
            pl.BlockSpec((depth, D), lambda b, c: (0, 0)),
            _layer_spec((1, HD), l),
            pl.BlockSpec(memory_space=pl.ANY),
        ],
        out_specs=[
            pl.BlockSpec((bb, C, D), lambda b, c: (b, c, 0)),
            pl.BlockSpec((1, bb, NH, HD, HD), lambda b, c: (l, b, 0, 0, 0)),
        ],
        out_shape=[
            jax.ShapeDtypeStruct((nb, t, D), F32),
            jax.ShapeDtypeStruct(s_buf.shape, F32),
        ],
        input_output_aliases={7: 1},
        scratch_shapes=[pltpu.VMEM((bb, NH, HD, HD), F32)],
        compiler_params=_cparams("parallel", "arbitrary"),
        name="mixer_b",
    )(proj3, proj3, proj3, proj3, s0, lb_param, norm_b, s_buf)


def _post_kernel(oa_ref, ob_ref, gt_ref, x_ref, g1_ref, sc2_ref, sh2_ref, wa_ref, wb_ref, wo_ref,
                 lg_ref, lb_ref, wr_ref, br_ref, x1_ref, h2_ref, rw_ref, re_ref, *, mode, alpha, nsub):
    sub = x_ref.shape[0] // nsub
    rows = [slice(k * sub, (k + 1) * sub) for k in range(nsub)]
    mod = lambda ref, r: ref[0, 0] if mode == "row" else ref[0, r, :]
    gates = [_sigmoid(gt_ref[r, :]) for r in rows]
    pa = [_dot(oa_ref[r, :], wa_ref[0]) for r in rows]
    pb = [_dot(ob_ref[r, :], wb_ref[0]) for r in rows]
    ys = [_dot(g[:, :D] * a + g[:, D:] * b, wo_ref[0]) for g, a, b in zip(gates, pa, pb)]
    x1s = [_layer_norm(alpha * x_ref[r, :] + (1.0 + mod(g1_ref, r)) * y, lg_ref[0], lb_ref[0])
           for r, y in zip(rows, ys)]
    h2s = [x1 * (1.0 + mod(sc2_ref, r)) + mod(sh2_ref, r) for r, x1 in zip(rows, x1s)]
    lgs = [_dot(h2, wr_ref[0], hi=True) + br_ref[0] for h2 in h2s]
    for r, x1, h2, logits in zip(rows, x1s, h2s, lgs):
        x1_ref[r, :] = x1
        h2_ref[r, :] = h2
        rw, re = _route(logits)
        rw_ref[r, :] = rw
        re_ref[r, :] = re


def _route(logits):
    lane = lax.broadcasted_iota(I32, logits.shape, 1)
    neg = jnp.float32(-jnp.inf)
    gl = jnp.where(lane < N_GROUPS, logits, neg)
    gmax = jnp.max(gl, -1, keepdims=True)
    grp = jnp.min(jnp.where(gl == gmax, lane, LANES), -1, keepdims=True)
    p_grp = 1.0 / jnp.sum(jnp.exp(gl - gmax), -1, keepdims=True)
    lo = N_GROUPS + grp * EPG
    el = jnp.where((lane >= lo) & (lane < lo + EPG), logits, neg)
    v1 = jnp.max(el, -1, keepdims=True)
    i1 = jnp.min(jnp.where(el == v1, lane, LANES), -1, keepdims=True)
    el2 = jnp.where(lane == i1, neg, el)
    v2 = jnp.max(el2, -1, keepdims=True)
    i2 = jnp.min(jnp.where(el2 == v2, lane, LANES), -1, keepdims=True)
    e2 = jnp.exp(v2 - v1)
    w1 = p_grp / (1.0 + e2)
    w2 = p_grp * e2 / (1.0 + e2)
    return (jnp.where(lane == 0, w1, jnp.where(lane == 1, w2, 0.0)),
            jnp.where(lane == 0, i1 - N_GROUPS, jnp.where(lane == 1, i2 - N_GROUPS, 0)))


def _post_mixer(o_a, o_b, proj, x, mods, w_br_a, w_br_b, w_o, ln_g, ln_b, w_r, b_r, l, mode, tm,
                tiles_per_batch, alpha):
    n = x.shape[0]
    tok = pl.BlockSpec((tm, D), lambda i: (i, 0))
    lane_out = pl.BlockSpec((tm, LANES), lambda i: (i, 0))
    return pl.pallas_call(
        functools.partial(_post_kernel, mode=mode, alpha=alpha, nsub=4),
        grid=(n // tm,),
        in_specs=[
            tok, tok,
            pl.BlockSpec((tm, 2 * D), lambda i: (i, C_MERGE // (2 * D))),
            tok,
            _mod_spec(mode, l, tm, 2, tiles_per_batch),
            _mod_spec(mode, l, tm, 4, tiles_per_batch),
            _mod_spec(mode, l, tm, 3, tiles_per_batch),
            _layer_spec((D, D), l), _layer_spec((D, D), l), _layer_spec((D, D), l),
            _layer_spec((1, D), l), _layer_spec((1, D), l),
            _layer_spec((D, LANES), l), _layer_spec((1, LANES), l),
        ],
        out_specs=[tok, tok, lane_out, lane_out],
        out_shape=[
            jax.ShapeDtypeStruct((n, D), F32),
            jax.ShapeDtypeStruct((n, D), F32),
            jax.ShapeDtypeStruct((n, LANES), F32),
            jax.ShapeDtypeStruct((n, LANES), I32),
        ],
        compiler_params=_cparams("parallel"),
        name="post_mixer",
    )(o_a, o_b, proj, x, mods, mods, mods, w_br_a, w_br_b, w_o, ln_g, ln_b, w_r, b_r)


TM_EXP = 256


def _dispatch_kernel(d0_ref, d1_ref, h_ref, xs_in, xs_out, sem, *, tm):
    del xs_in

    def copies(r):
        src = h_ref.at[pl.ds(r, 1), :]
        return (pltpu.make_async_copy(src, xs_out.at[pl.ds(d0_ref[r], 1), :], sem),
                pltpu.make_async_copy(src, xs_out.at[pl.ds(d1_ref[r], 1), :], sem))

    def start(r, carry):
        a, b = copies(r)
        a.start()
        b.start()
        return carry

    def wait(r, carry):
        a, b = copies(r)
        a.wait()
        b.wait()
        return carry

    lax.fori_loop(0, tm, start, 0, unroll=4)
    lax.fori_loop(0, tm, wait, 0, unroll=4)


def _dispatch(h2, d0, d1, xs_buf, tm):
    n = h2.shape[0]
    return pl.pallas_call(
        functools.partial(_dispatch_kernel, tm=tm),
        grid=(n // tm,),
        in_specs=[
            pl.BlockSpec((tm,), lambda i: (i,), memory_space=pltpu.SMEM),
            pl.BlockSpec((tm,), lambda i: (i,), memory_space=pltpu.SMEM),
            pl.BlockSpec((tm, D), lambda i: (i, 0)),
            pl.BlockSpec(memory_space=pl.ANY),
        ],
        out_specs=pl.BlockSpec(memory_space=pl.ANY),
        out_shape=jax.ShapeDtypeStruct(xs_buf.shape, F32),
        input_output_aliases={3: 0},
        scratch_shapes=[pltpu.SemaphoreType.DMA],
        compiler_params=_cparams("arbitrary"),
        name="dispatch",
    )(d0, d1, h2, xs_buf)


def _expert_kernel(be_ref, nblk_ref, xs_ref, wg_ref, wu_ref, wd_ref, ys_ref):
    i = pl.program_id(0)

    @pl.when(i < nblk_ref[0])
    def _():
        wg, wu, wd = wg_ref[0, 0].astype(BF16), wu_ref[0, 0].astype(BF16), wd_ref[0, 0].astype(BF16)
        half = TM_EXP // 2
        rows = [slice(0, half), slice(half, TM_EXP)]
        xh = [xs_ref[r, :].astype(BF16) for r in rows]
        gs = [jnp.dot(x, wg, preferred_element_type=F32) for x in xh]
        us = [jnp.dot(x, wu, preferred_element_type=F32) for x in xh]
        hs = [(_silu(g) * u).astype(BF16) for g, u in zip(gs, us)]
        ys = [jnp.dot(h, wd, preferred_element_type=F32) for h in hs]
        for r, y in zip(rows, ys):
            ys_ref[r, :] = y

    @pl.when(i >= nblk_ref[0])
    def _():
        ys_ref[...] = jnp.zeros(ys_ref.shape, F32)


def _experts(xs, block_e, n_used, w_gate, w_up, w_down, l):
    n_rows = xs.shape[0]
    n_blocks = n_rows // TM_EXP
    grid_spec = pltpu.PrefetchScalarGridSpec(
        num_scalar_prefetch=2,
        grid=(n_blocks,),
        in_specs=[
            pl.BlockSpec((TM_EXP, D), lambda i, be, nb: (i, 0)),
            pl.BlockSpec((1, 1, D, D_EXP), lambda i, be, nb: (l, be[i], 0, 0)),
            pl.BlockSpec((1, 1, D, D_EXP), lambda i, be, nb: (l, be[i], 0, 0)),
            pl.BlockSpec((1, 1, D_EXP, D), lambda i, be, nb: (l, be[i], 0, 0)),
        ],
        out_specs=pl.BlockSpec((TM_EXP, D), lambda i, be, nb: (i, 0)),
    )
    return pl.pallas_call(
        _expert_kernel,
        grid_spec=grid_spec,
        out_shape=jax.ShapeDtypeStruct((n_rows, D), F32),
        compiler_params=_cparams("arbitrary"),
        name="experts",
    )(block_e, n_used, xs, w_gate, w_up, w_down)


def _combine_kernel(d0_ref, d1_ref, ys_hbm, rw_ref, x_ref, g2_ref, lg_ref, lb_ref, o_ref,
                    y0, y1, sem, *, mode, alpha, tm):
    def copies(r):
        return (pltpu.make_async_copy(ys_hbm.at[pl.ds(d0_ref[r], 1), :], y0.at[pl.ds(r, 1), :], sem),
                pltpu.make_async_copy(ys_hbm.at[pl.ds(d1_ref[r], 1), :], y1.at[pl.ds(r, 1), :], sem))

    def start(r, carry):
        a, b = copies(r)
        a.start()
        b.start()
        return carry

    def wait(r, carry):
        a, b = copies(r)
        a.wait()
        b.wait()
        return carry

    lax.fori_loop(0, tm, start, 0, unroll=4)
    lax.fori_loop(0, tm, wait, 0, unroll=4)
    rw = rw_ref[...]
    y = rw[:, 0:1] * y0[...] + rw[:, 1:2] * y1[...]
    o_ref[...] = _layer_norm(alpha * x_ref[...] + (1.0 + _mod_val(g2_ref, mode)) * y,
                             lg_ref[0], lb_ref[0])


def _combine(ys, d0, d1, rw, x1, mods, ln_g, ln_b, l, mode, tm, tiles_per_batch, alpha):
    n = x1.shape[0]
    return pl.pallas_call(
        functools.partial(_combine_kernel, mode=mode, alpha=alpha, tm=tm),
        grid=(n // tm,),
        in_specs=[
            pl.BlockSpec((tm,), lambda i: (i,), memory_space=pltpu.SMEM),
            pl.BlockSpec((tm,), lambda i: (i,), memory_space=pltpu.SMEM),
            pl.BlockSpec(memory_space=pl.ANY),
            pl.BlockSpec((tm, LANES), lambda i: (i, 0)),
            pl.BlockSpec((tm, D), lambda i: (i, 0)),
            _mod_spec(mode, l, tm, 5, tiles_per_batch),
            _layer_spec((1, D), l),
            _layer_spec((1, D), l),
        ],
        out_specs=pl.BlockSpec((tm, D), lambda i: (i, 0)),
        out_shape=jax.ShapeDtypeStruct((n, D), F32),
        scratch_shapes=[pltpu.VMEM((tm, D), F32), pltpu.VMEM((tm, D), F32), pltpu.SemaphoreType.DMA],
        compiler_params=_cparams("arbitrary"),
        name="combine",
    )(d0, d1, ys, rw, x1, mods, ln_g, ln_b)


def _route_meta(eid):
    n = eid.shape[0]
    flat = eid.reshape(-1)
    n_slots = flat.shape[0]
    onehot = (flat[:, None] == jnp.arange(N_EXP, dtype=I32)[None, :]).astype(I32)
    csum = jnp.cumsum(onehot, axis=0)
    counts = csum[-1]
    rank = jnp.take_along_axis(csum - onehot, flat[:, None], axis=1)[:, 0]
    padded = (counts + TM_EXP - 1) // TM_EXP * TM_EXP
    pad_end = jnp.cumsum(padded)
    pad_start = pad_end - padded
    dest = pad_start[flat] + rank
    n_blocks = (n_slots + N_EXP * (TM_EXP - 1)) // TM_EXP
    block_start = jnp.arange(n_blocks, dtype=I32) * TM_EXP
    block_e = jnp.minimum(jnp.sum(pad_end[None, :] <= block_start[:, None], axis=1), N_EXP - 1).astype(I32)
    n_used = (pad_end[-1:] // TM_EXP).astype(I32)
    dest = dest.reshape(n, 2)
    return n_blocks * TM_EXP, block_e, n_used, dest[:, 0], dest[:, 1]


def _pack_w_in(w_in):
    off_g_a = CONV_DIM
    off_decay = off_g_a + D
    off_q_b = off_decay + 2 * NH
    off_merge = off_q_b + 4 * D
    depth = w_in.shape[0]
    pad = jnp.zeros((depth, D, LANES - 2 * NH), w_in.dtype)
    small = jnp.concatenate([w_in[:, :, off_decay:off_q_b], pad], axis=2)
    main = jnp.concatenate([w_in[:, :, :off_decay], w_in[:, :, off_q_b:off_merge + 2 * D]], axis=2)
    return main.astype(BF16), small.astype(BF16)


def _pad_lanes(v):
    depth, k = v.shape
    return jnp.concatenate([v, jnp.zeros((depth, LANES - k), v.dtype)], axis=1).reshape(depth, 1, LANES)


def kernel(x_prompt, x_sample, state_conv, state_delta, state_hgrn, c_prompt, c_sample, ln0_g, ln0_b, w_ada, b_ada, w_in, conv_w, a_log, dt_bias, norm_a, lb_param, norm_b, w_br_a, w_br_b, w_o, ln1_g, ln1_b, w_grp, b_grp, w_rt, b_rt, w_gate_e, w_up_e, w_down_e, ln2_g, ln2_b):
    depth = w_in.shape[0]
    alpha = (2 * depth) ** 0.25
    bp, tp, _ = x_prompt.shape
    bs, ts, _ = x_sample.shape
    n_p = bp * tp
    tsp = SUBLANES
    assert ts <= tsp
    n_s = bs * tsp

    w_main, w_small = _pack_w_in(w_in)
    w_a16, w_b16, w_o16 = w_br_a.astype(BF16), w_br_b.astype(BF16), w_o.astype(BF16)
    w_r = jnp.concatenate([w_grp, w_rt, jnp.zeros((depth, D, LANES - N_GROUPS - N_EXP), F32)], axis=2)
    b_r = jnp.concatenate([b_grp, b_rt, jnp.zeros((depth, LANES - N_GROUPS - N_EXP), F32)], axis=1)
    b_r = b_r.reshape(depth, 1, LANES)
    a_log_p, dt_bias_p = _pad_lanes(a_log), _pad_lanes(dt_bias)
    norm_a3, norm_b3 = norm_a.reshape(depth, 1, HD), norm_b.reshape(depth, 1, HD)
    ln1_g3, ln1_b3 = ln1_g.reshape(depth, 1, D), ln1_b.reshape(depth, 1, D)
    ln2_g3, ln2_b3 = ln2_g.reshape(depth, 1, D), ln2_b.reshape(depth, 1, D)

    mods = _modulation(jnp.concatenate([c_prompt, c_sample], axis=0), w_ada, b_ada)
    mods_p = mods[:, :bp].reshape(depth, bp, 1, 6 * D)
    mods_s = jnp.repeat(mods[:, bp:], tsp, axis=1)

    xs_pad = jnp.concatenate([x_sample, jnp.zeros((bs, tsp - ts, D), x_sample.dtype)], axis=1)
    cp = math.gcd(tp, 128)
    tm_p = math.gcd(tp, 512)
    tm_s = math.gcd(n_s, 512)
    bb_s = math.gcd(bs, 4)
    state_shape = lambda nb: (depth, nb, NH, HD, HD)
    conv_shape = lambda nb: (depth, nb, CONV_W - 1, CONV_DIM)
    groups = [
        dict(mode="row", n=n_p, nb=bp, t=tp, C=cp, tv=cp, tm=tm_p, tpb=tp // tm_p, bb=1, mods=mods_p,
             conv=jnp.zeros(conv_shape(bp), F32), delta=jnp.zeros(state_shape(bp), F32),
             hgrn=jnp.zeros(state_shape(bp), F32)),
        dict(mode="tok", n=n_s, nb=bs, t=tsp, C=tsp, tv=ts, tm=tm_s, tpb=1, bb=bb_s, mods=mods_s,
             conv=state_conv, delta=state_delta, hgrn=state_hgrn),
    ]
    for g in groups:
        g["conv_o"] = jnp.zeros(conv_shape(g["nb"]), F32)
        g["delta_o"] = jnp.zeros(state_shape(g["nb"]), F32)
        g["hgrn_o"] = jnp.zeros(state_shape(g["nb"]), F32)
    xs = [_ln0(x_prompt.reshape(n_p, D), ln0_g, ln0_b, tm_p),
          _ln0(xs_pad.reshape(n_s, D), ln0_g, ln0_b, tm_s)]

    for l in range(depth):
        x1s, h2s, rws, res = [], [], [], []
        for gi, g in enumerate(groups):
            tm_in = math.gcd(g["t"] if g["mode"] == "row" else g["n"], 1024)
            proj, small = _inproj(xs[gi], g["mods"], w_main, w_small, l, g["mode"], tm_in, g["t"] // tm_in)
            proj3 = proj.reshape(g["nb"], g["t"], NPROJ)
            small3 = small.reshape(g["nb"], g["t"], LANES)
            o_a, g["conv_o"], g["delta_o"] = _mixer_a(
                proj3, small3, g["conv"], g["delta"], g["conv_o"], g["delta_o"], conv_w, a_log_p, dt_bias_p,
                norm_a3, l, g["C"], g["tv"], g["bb"])
            o_b, g["hgrn_o"] = _mixer_b(proj3, g["hgrn"], g["hgrn_o"], lb_param, norm_b3, l, g["C"],
                                        g["tv"], g["bb"])
            halve = 1 if g["mode"] == "row" else 2
            x1, h2, rw, re = _post_mixer(o_a.reshape(g["n"], D), o_b.reshape(g["n"], D), proj, xs[gi],
                                         g["mods"], w_a16, w_b16, w_o16, ln1_g3, ln1_b3, w_r, b_r, l,
                                         g["mode"], g["tm"] // halve, g["tpb"] * halve, alpha)
            x1s.append(x1)
            h2s.append(h2)
            rws.append(rw)
            res.append(re)
        eid = jnp.concatenate([r[:, :2] for r in res], axis=0)
        n_rows, block_e, n_used, d0, d1 = _route_meta(eid)
        if l == 0:
            xs_sorted = jnp.zeros((n_rows, D), F32)
        off = 0
        for gi, g in enumerate(groups):
            sl = slice(off, off + g["n"])
            xs_sorted = _dispatch(h2s[gi], d0[sl], d1[sl], xs_sorted, g["tm"])
            off += g["n"]
        ys = _experts(xs_sorted, block_e, n_used, w_gate_e, w_up_e, w_down_e, l)
        off = 0
        for gi, g in enumerate(groups):
            sl = slice(off, off + g["n"])
            xs[gi] = _combine(ys, d0[sl], d1[sl], rws[gi], x1s[gi], g["mods"], ln2_g3, ln2_b3, l,
                              g["mode"], g["tm"], g["tpb"], alpha)
            off += g["n"]

    y_prompt = xs[0].reshape(bp, tp, D)
    y_sample = xs[1].reshape(bs, tsp, D)[:, :ts]
    gp, gs = groups
    return (y_prompt, y_sample, gp["conv_o"], gp["delta_o"], gp["hgrn_o"],
            gs["conv_o"], gs["delta_o"], gs["hgrn_o"])
```

```python
import functools
import math

import jax
import jax.numpy as jnp
from jax import lax
from jax.experimental import pallas as pl
from jax.experimental.pallas import tpu as pltpu

F32 = jnp.float32
BF16 = jnp.bfloat16
I32 = jnp.int32

D = 1024
NH = 8
HD = 128
CONV_W = 4
CONV_DIM = 3 * D
N_GROUPS = 4
EPG = 8
N_EXP = N_GROUPS * EPG
D_EXP = 512
F_MIN = 1e-30
LN_EPS = 1e-5
RMS_EPS = 1e-6

C_QKV = 0
C_GA = 3 * D
C_QB = 4 * D
C_FB = 5 * D
C_IB = 6 * D
C_GB = 7 * D
C_MERGE = 8 * D
NPROJ = 10 * D

LANES = 128
SUBLANES = 8
VMEM_LIMIT = 48 * 1024 * 1024

HI = lax.Precision.HIGHEST


def _cparams(*sem):
    return pltpu.CompilerParams(dimension_semantics=sem, vmem_limit_bytes=VMEM_LIMIT)


def _dot(a, b, hi=False):
    if hi:
        return jnp.dot(a, b, preferred_element_type=F32, precision=HI)
    return jnp.dot(a.astype(BF16), b.astype(BF16), preferred_element_type=F32)


def _dot_nt(a, b):
    return lax.dot_general(a.astype(BF16), b.astype(BF16), (((1,), (1,)), ((), ())),
                           preferred_element_type=F32)


def _dot_tn(a, b):
    return lax.dot_general(a.astype(BF16), b.astype(BF16), (((0,), (0,)), ((), ())),
                           preferred_element_type=F32)


def _split(a):
    hi = a.astype(BF16)
    return hi, (a - hi.astype(F32)).astype(BF16)


def _dot3(a, b):
    mm = lambda x, y: jnp.dot(x, y, preferred_element_type=F32)
    return mm(a[0], b[0]) + (mm(a[0], b[1]) + mm(a[1], b[0]))


def _sigmoid(x):
    return 1.0 / (1.0 + jnp.exp(-x))


def _silu(x):
    return x * _sigmoid(x)


def _softplus(x):
    return jnp.maximum(x, 0.0) + jnp.log(1.0 + jnp.exp(-jnp.abs(x)))


def _layer_norm(x, g, b):
    mu = jnp.mean(x, -1, keepdims=True)
    xc = x - mu
    var = jnp.mean(xc * xc, -1, keepdims=True)
    return xc * lax.rsqrt(var + LN_EPS) * g + b


def _layer_spec(shape, l):
    zeros = (0,) * len(shape)
    return pl.BlockSpec((1,) + tuple(shape), lambda *_: (l,) + zeros)


def _mod_kernel(c_ref, w_ref, b_ref, o_ref):
    ca = _silu(c_ref[...])
    o_ref[0] = _dot(ca, w_ref[0]) + b_ref[0]


def _modulation(c_all, w_ada, b_ada):
    depth = w_ada.shape[0]
    nb = c_all.shape[0]
    tn = 1536
    return pl.pallas_call(
        _mod_kernel,
        grid=(depth, 6 * D // tn),
        in_specs=[
            pl.BlockSpec((nb, D), lambda l, j: (0, 0)),
            pl.BlockSpec((1, D, tn), lambda l, j: (l, 0, j)),
            pl.BlockSpec((1, 1, tn), lambda l, j: (l, 0, j)),
        ],
        out_specs=pl.BlockSpec((1, nb, tn), lambda l, j: (l, 0, j)),
        out_shape=jax.ShapeDtypeStruct((depth, nb, 6 * D), F32),
        compiler_params=_cparams("parallel", "parallel"),
        name="adaln_mod",
    )(c_all, w_ada, b_ada.reshape(depth, 1, 6 * D))


def _mod_spec(mode, l, tm, k, tiles_per_batch):
    if mode == "row":
        return pl.BlockSpec((1, 1, 1, D), lambda i, *_: (l, i // tiles_per_batch, 0, k))
    return pl.BlockSpec((1, tm, D), lambda i, *_: (l, i, k))


def _mod_val(ref, mode):
    return ref[0, 0] if mode == "row" else ref[0]


def _ln0_kernel(x_ref, g_ref, b_ref, o_ref):
    o_ref[...] = _layer_norm(x_ref[...], g_ref[...], b_ref[...])


def _ln0(x, g, b, tm):
    n = x.shape[0]
    return pl.pallas_call(
        _ln0_kernel,
        grid=(n // tm,),
        in_specs=[pl.BlockSpec((tm, D), lambda i: (i, 0)),
                  pl.BlockSpec((1, D), lambda i: (0, 0)),
                  pl.BlockSpec((1, D), lambda i: (0, 0))],
        out_specs=pl.BlockSpec((tm, D), lambda i: (i, 0)),
        out_shape=jax.ShapeDtypeStruct((n, D), F32),
        compiler_params=_cparams("parallel"),
        name="ln0",
    )(x, g.reshape(1, D), b.reshape(1, D))


def _inproj_kernel(x_ref, sc_ref, sh_ref, w_ref, ws_ref, o_ref, os_ref, h_scr, *, mode):
    @pl.when(pl.program_id(1) == 0)
    def _():
        h = x_ref[...] * (1.0 + _mod_val(sc_ref, mode)) + _mod_val(sh_ref, mode)
        h_scr[...] = h.astype(BF16)
        os_ref[...] = jnp.dot(h_scr[...], ws_ref[0], preferred_element_type=F32).reshape(os_ref.shape)

    o_ref[...] = jnp.dot(h_scr[...], w_ref[0], preferred_element_type=F32).reshape(o_ref.shape)


def _inproj(x, mods, w_main, w_small, l, mode, tm, tiles_per_batch, t3=None):
    n = x.shape[0]
    tn = 1280
    if t3 is None:
        out_specs = [pl.BlockSpec((tm, tn), lambda i, j: (i, j)),
                     pl.BlockSpec((tm, LANES), lambda i, j: (i, 0))]
        out_shape = [jax.ShapeDtypeStruct((n, NPROJ), F32), jax.ShapeDtypeStruct((n, LANES), F32)]
    else:
        out_specs = [pl.BlockSpec((tm // t3, t3, tn), lambda i, j: (i, 0, j)),
                     pl.BlockSpec((tm // t3, t3, LANES), lambda i, j: (i, 0, 0))]
        out_shape = [jax.ShapeDtypeStruct((n // t3, t3, NPROJ), F32),
                     jax.ShapeDtypeStruct((n // t3, t3, LANES), F32)]
    return pl.pallas_call(
        functools.partial(_inproj_kernel, mode=mode),
        grid=(n // tm, NPROJ // tn),
        in_specs=[
            pl.BlockSpec((tm, D), lambda i, j: (i, 0)),
            _mod_spec(mode, l, tm, 1, tiles_per_batch),
            _mod_spec(mode, l, tm, 0, tiles_per_batch),
            pl.BlockSpec((1, D, tn), lambda i, j: (l, 0, j)),
            pl.BlockSpec((1, D, LANES), lambda i, j: (l, 0, 0)),
        ],
        out_specs=out_specs,
        out_shape=out_shape,
        scratch_shapes=[pltpu.VMEM((tm, D), BF16)],
        compiler_params=_cparams("parallel", "arbitrary"),
        name="inproj",
    )(x, mods, mods, w_main, w_small)


def _mixa_kernel(qkv_ref, ga_ref, sm_ref, cst_ref, s0_ref, cw_ref, alog_ref, dtb_ref, na_ref,
                 cbuf_ref, sbuf_ref, oa_ref, cso_ref, so_ref, xbuf, s_scr, *, C, tv, nc, bb):
    del cbuf_ref, sbuf_ref
    c = pl.program_id(1)
    row = lax.broadcasted_iota(I32, (C, 1), 0)
    ri = lax.broadcasted_iota(I32, (C, C), 0)
    ci = lax.broadcasted_iota(I32, (C, C), 1)
    causal = ri >= ci
    strict = ri > ci
    eye = (ri == ci).astype(F32)
    tril = causal.astype(F32)
    hist = SUBLANES - (CONV_W - 1)

    @pl.when(c == 0)
    def _():
        for bi in range(bb):
            xbuf[bi, 0:SUBLANES, :] = jnp.zeros((SUBLANES, CONV_DIM), F32)
            xbuf[bi, hist:SUBLANES, :] = cst_ref[0, bi]
            s_scr[bi] = s0_ref[0, bi]

    pre = []
    for bi in range(bb):
        xbuf[bi, SUBLANES:SUBLANES + C, :] = qkv_ref[bi]
        sm = sm_ref[bi]
        g_all = -jnp.exp(alog_ref[0]) * _softplus(sm + dtb_ref[0])
        beta_all = _sigmoid(sm)
        if tv < C:
            g_all = jnp.where(row < tv, g_all, 0.0)
            beta_all = jnp.where(row < tv, beta_all, 0.0)
        gcum_all = _dot(tril, g_all, hi=True)
        pre.append((gcum_all, gcum_all.T, jnp.exp(gcum_all), beta_all))

    n_nil = min(C, tv)
    n_double = max(int(math.ceil(math.log2(n_nil))) - 1, 0)
    units = [(bi, h) for bi in range(bb) for h in range(NH)]

    st = []
    for bi, h in units:
        gcum_all, gcum_t, eg_all, beta_all = pre[bi]
        conv = []
        for part in range(3):
            col = slice(part * D + h * HD, part * D + (h + 1) * HD)
            acc = None
            for j in range(CONV_W):
                term = xbuf[bi, hist + j:hist + j + C, col] * cw_ref[0, j:j + 1, col]
                acc = term if acc is None else acc + term
            conv.append(_silu(acc))
        q, k, v = conv
        q = q * lax.rsqrt(jnp.sum(q * q, -1, keepdims=True) + RMS_EPS) * (HD ** -0.5)
        k = k * lax.rsqrt(jnp.sum(k * k, -1, keepdims=True) + RMS_EPS)
        gc = gcum_all[:, h:h + 1]
        gr = gcum_t[h:h + 1, :]
        bt = beta_all[:, NH + h:NH + h + 1]
        eg = eg_all[:, h:h + 1]
        glast = gcum_all[tv - 1:tv, h:h + 1]
        dec = jnp.where(causal, jnp.exp(jnp.where(causal, gc - gr, 0.0)), 0.0)
        kb = k * bt
        kk = _dot_nt(jnp.concatenate([kb, q], axis=0), k)
        st.append(dict(
            a=-jnp.where(strict, kk[:C] * dec, 0.0),
            rhs=jnp.concatenate([kb * eg, v * bt], axis=1),
            qk=jnp.where(causal, kk[C:] * dec, 0.0),
            qe=q * eg,
            kg=k * jnp.exp(glast - gc),
            eglast=jnp.exp(glast),
        ))

    ps = [eye + u["a"] for u in st]
    qsp = [_split(u["a"]) for u in st]
    for _ in range(n_double):
        qs = [_dot3(s2, s2) for s2 in qsp]
        qsp = [_split(x) for x in qs]
        ps = [p + _dot3(_split(p), s2) for p, s2 in zip(ps, qsp)]
    sols = [_dot3(_split(p), _split(u["rhs"])) for p, u in zip(ps, st)]

    ss = [s_scr[bi, h] for bi, h in units]
    gates = [_silu(ga_ref[bi, :, h * HD:(h + 1) * HD]) for bi, h in units]
    wss = [_dot(jnp.concatenate([sol[:, :HD], u["qe"]], axis=0), s)
           for u, sol, s in zip(st, sols, ss)]
    vns = [sol[:, HD:] - ws[:C] for sol, ws in zip(sols, wss)]
    os_ = [ws[C:] + _dot(u["qk"], vn) for u, ws, vn in zip(st, wss, vns)]
    s_new = [u["eglast"] * s + _dot_tn(u["kg"], vn) for u, s, vn in zip(st, ss, vns)]
    ys = [o * lax.rsqrt(jnp.mean(o * o, -1, keepdims=True) + RMS_EPS) * na_ref[0] * gt
          for o, gt in zip(os_, gates)]
    for (bi, h), sn, y in zip(units, s_new, ys):
        s_scr[bi, h] = sn
        oa_ref[bi, :, h * HD:(h + 1) * HD] = y

    if nc > 1:
        for bi in range(bb):
            xbuf[bi, 0:SUBLANES, :] = xbuf[bi, C:C + SUBLANES, :]

    @pl.when(c == nc - 1)
    def _():
        for bi in range(bb):
            if nc > 1:
                cso_ref[0, bi] = xbuf[bi, hist:SUBLANES, :]
            else:
                cso_ref[0, bi] = xbuf[bi, SUBLANES + tv - (CONV_W - 1):SUBLANES + tv, :]
            so_ref[0, bi] = s_scr[bi]


def _mixer_a(proj3, small3, conv_state, s0, conv_buf, s_buf, conv_w, a_log_p, dt_bias_p, norm_a, l, C, tv,
             bb):
    nb, t, _ = proj3.shape
    nc = t // C
    assert nc == 1 or tv == C
    return pl.pallas_call(
        functools.partial(_mixa_kernel, C=C, tv=tv, nc=nc, bb=bb),
        grid=(nb // bb, nc),
        in_specs=[
            pl.BlockSpec((bb, C, CONV_DIM), lambda b, c: (b, c, C_QKV // CONV_DIM)),
            pl.BlockSpec((bb, C, D), lambda b, c: (b, c, C_GA // D)),
            pl.BlockSpec((bb, C, LANES), lambda b, c: (b, c, 0)),
            pl.BlockSpec((1, bb, CONV_W - 1, CONV_DIM), lambda b, c: (l, b, 0, 0)),
            pl.BlockSpec((1, bb, NH, HD, HD), lambda b, c: (l, b, 0, 0, 0)),
            _layer_spec((CONV_W, CONV_DIM), l),
            _layer_spec((1, LANES), l),
            _layer_spec((1, LANES), l),
            _layer_spec((1, HD), l),
            pl.BlockSpec(memory_space=pl.ANY),
            pl.BlockSpec(memory_space=pl.ANY),
        ],
        out_specs=[
            pl.BlockSpec((bb, C, D), lambda b, c: (b, c, 0)),
            pl.BlockSpec((1, bb, CONV_W - 1, CONV_DIM), lambda b, c: (l, b, 0, 0)),
            pl.BlockSpec((1, bb, NH, HD, HD), lambda b, c: (l, b, 0, 0, 0)),
        ],
        out_shape=[
            jax.ShapeDtypeStruct((nb, t, D), F32),
            jax.ShapeDtypeStruct(conv_buf.shape, F32),
            jax.ShapeDtypeStruct(s_buf.shape, F32),
        ],
        input_output_aliases={9: 1, 10: 2},
        scratch_shapes=[pltpu.VMEM((bb, C + 2 * SUBLANES, CONV_DIM), F32),
                        pltpu.VMEM((bb, NH, HD, HD), F32)],
        compiler_params=_cparams("parallel", "arbitrary"),
        name="mixer_a",
    )(proj3, proj3, small3, conv_state, s0, conv_w, a_log_p, dt_bias_p, norm_a, conv_buf, s_buf)


DIAG = SUBLANES


def _mixb_kernel(qb_ref, fb_ref, ib_ref, gb_ref, s0_ref, lbp_ref, nb_ref, sbuf_ref,
                 ob_ref, so_ref, st_scr, *, C, tv, nc, bb, layer):
    del sbuf_ref
    c = pl.program_id(1)
    row = lax.broadcasted_iota(I32, (C, 1), 0)
    ri = lax.broadcasted_iota(I32, (C, C), 0)
    ci = lax.broadcasted_iota(I32, (C, C), 1)
    tril = (ri >= ci).astype(F32)

    @pl.when(c == 0)
    def _():
        for bi in range(bb):
            for h in range(NH):
                st_scr[bi, h] = s0_ref[0, bi, h].T

    lbp = lbp_ref[...]
    e = jnp.exp(lbp - jnp.max(lbp, axis=0, keepdims=True))
    psm = e / jnp.sum(e, axis=0, keepdims=True)
    lb = jnp.zeros((1, D), F32)
    for j in range(1, layer + 1):
        lb = lb + psm[j:j + 1, :]

    row8 = lax.broadcasted_iota(I32, (DIAG, 1), 0)
    levels = []
    lv = DIAG
    while lv < C:
        levels.append(lv)
        lv *= 2

    units = []
    for bi in range(bb):
        z = fb_ref[bi]
        f = lb + (1.0 - lb) * _sigmoid(z)
        lf_all = jnp.log(jnp.maximum(f, F_MIN))
        kk_all = (1.0 - lb) * _sigmoid(-z)
        if tv < C:
            lf_all = jnp.where(row < tv, lf_all, 0.0)
            kk_all = jnp.where(row < tv, kk_all, 0.0)
        b_all = _dot(tril, lf_all, hi=True)

        for h in range(NH):
            cs = slice(h * HD, (h + 1) * HD)
            q = _silu(qb_ref[bi, :, cs])
            kk = kk_all[:, cs]
            iv = ib_ref[bi, :, cs]
            b = b_all[:, cs]
            blast = b[tv - 1:tv, :]

            amat = None
            for lv in levels:
                pieces = []
                for j in range(C // lv):
                    blk = b[j * lv:(j + 1) * lv, :]
                    if j % 2 == 1:
                        pieces.append(blk - b[j * lv:j * lv + 1, :])
                    else:
                        pieces.append(b[(j + 1) * lv:(j + 1) * lv + 1, :] - blk)
                ex = jnp.exp(jnp.concatenate(pieces, axis=0))
                lower = ((row // lv) % 2) == 1
                qs = jnp.where(lower, q * ex, 0.0)
                ks = jnp.where(lower, 0.0, kk * ex)
                blk_a = jnp.where((ri // (2 * lv)) == (ci // (2 * lv)), _dot_nt(qs, ks), 0.0)
                amat = blk_a if amat is None else amat + blk_a

            od = []
            for j in range(C // DIAG):
                rs = slice(j * DIAG, (j + 1) * DIAG)
                bd, qq, k8, i8 = b[rs, :], q[rs, :], kk[rs, :], iv[rs, :]
                acc = jnp.zeros((DIAG, HD), F32)
                for s in range(DIAG):
                    ex = jnp.exp(jnp.where(row8 >= s, bd - bd[s:s + 1, :], -jnp.inf))
                    zz = qq * ex * k8[s:s + 1, :]
                    acc = acc + jnp.sum(zz, -1, keepdims=True) * i8[s:s + 1, :]
                od.append(acc)
            o = od[0] if len(od) == 1 else jnp.concatenate(od, axis=0)
            if amat is not None:
                o = o + _dot(amat, iv)

            units.append(dict(bi=bi, h=h, o=o, qe=q * jnp.exp(b), kg=kk * jnp.exp(blast - b), iv=iv,
                              eb=jnp.exp(blast), gate=_silu(gb_ref[bi, :, cs])))

    sts = [st_scr[u["bi"], u["h"]] for u in units]
    os_ = [u["o"] + _dot_nt(u["qe"], st) for u, st in zip(units, sts)]
    st_new = [st * u["eb"] + _dot_tn(u["iv"], u["kg"]) for u, st in zip(units, sts)]
    ys = [o * lax.rsqrt(jnp.mean(o * o, -1, keepdims=True) + RMS_EPS) * nb_ref[0] * u["gate"]
          for u, o in zip(units, os_)]
    for u, sn, y in zip(units, st_new, ys):
        st_scr[u["bi"], u["h"]] = sn
        ob_ref[u["bi"], :, u["h"] * HD:(u["h"] + 1) * HD] = y

    @pl.when(c == nc - 1)
    def _():
        for bi in range(bb):
            for h in range(NH):
                so_ref[0, bi, h] = st_scr[bi, h].T


def _mixer_b(proj3, s0, s_buf, lb_param, norm_b, l, C, tv, bb):
    nb, t, _ = proj3.shape
    nc = t // C
    depth = lb_param.shape[0]
    tok = lambda col: pl.BlockSpec((bb, C, D), lambda b, c: (b, c, col // D))
    return pl.pallas_call(
        functools.partial(_mixb_kernel, C=C, tv=tv, nc=nc, bb=bb, layer=l),
        grid=(nb // bb, nc),
        in_specs=[
            tok(C_QB), tok(C_FB), tok(C_IB), tok(C_GB),
            pl.BlockSpec((1, bb, NH, HD, HD), lambda b, c: (l, b, 0, 0, 0)),
            pl.BlockSpec((depth, D), lambda b, c: (0, 0)),
            _layer_spec((1, HD), l),
            pl.BlockSpec(memory_space=pl.ANY),
        ],
        out_specs=[
            pl.BlockSpec((bb, C, D), lambda b, c: (b, c, 0)),
            pl.BlockSpec((1, bb, NH, HD, HD), lambda b, c: (l, b, 0, 0, 0)),
        ],
        out_shape=[
            jax.ShapeDtypeStruct((nb, t, D), F32),
            jax.ShapeDtypeStruct(s_buf.shape, F32),
        ],
        input_output_aliases={7: 1},
        scratch_shapes=[pltpu.VMEM((bb, NH, HD, HD), F32)],
        compiler_params=_cparams("parallel", "arbitrary"),
        name="mixer_b",
    )(proj3, proj3, proj3, proj3, s0, lb_param, norm_b, s_buf)


def _post_kernel(oa_ref, ob_ref, gt_ref, x_ref, g1_ref, sc2_ref, sh2_ref, wa_ref, wb_ref, wo_ref,
                 lg_ref, lb_ref, wr_ref, br_ref, x1_ref, h2_ref, rw_ref, re_ref, *, mode, alpha, nsub):
    sub = x_ref.shape[0] // nsub
    rows = [slice(k * sub, (k + 1) * sub) for k in range(nsub)]
    mod = lambda ref, r: ref[0, 0] if mode == "row" else ref[0, r, :]

    def tok(ref, k):
        if len(ref.shape) == 2:
            return ref[rows[k], :]
        t3 = ref.shape[1]
        return ref[k * sub // t3:(k + 1) * sub // t3].reshape(sub, ref.shape[2])

    gates = [_sigmoid(tok(gt_ref, k)) for k in range(nsub)]
    pa = [_dot(tok(oa_ref, k), wa_ref[0]) for k in range(nsub)]
    pb = [_dot(tok(ob_ref, k), wb_ref[0]) for k in range(nsub)]
    ys = [_dot(g[:, :D] * a + g[:, D:] * b, wo_ref[0]) for g, a, b in zip(gates, pa, pb)]
    x1s = [_layer_norm(alpha * x_ref[r, :] + (1.0 + mod(g1_ref, r)) * y, lg_ref[0], lb_ref[0])
           for r, y in zip(rows, ys)]
    h2s = [x1 * (1.0 + mod(sc2_ref, r)) + mod(sh2_ref, r) for r, x1 in zip(rows, x1s)]
    lgs = [_dot(h2, wr_ref[0], hi=True) + br_ref[0] for h2 in h2s]
    for r, x1, h2, logits in zip(rows, x1s, h2s, lgs):
        x1_ref[r, :] = x1
        h2_ref[r, :] = h2
        rw, re = _route(logits)
        rw_ref[r, :] = rw
        re_ref[r, :] = re


def _route(logits):
    lane = lax.broadcasted_iota(I32, logits.shape, 1)
    neg = jnp.float32(-jnp.inf)
    gl = jnp.where(lane < N_GROUPS, logits, neg)
    gmax = jnp.max(gl, -1, keepdims=True)
    grp = jnp.min(jnp.where(gl == gmax, lane, LANES), -1, keepdims=True)
    p_grp = 1.0 / jnp.sum(jnp.exp(gl - gmax), -1, keepdims=True)
    lo = N_GROUPS + grp * EPG
    el = jnp.where((lane >= lo) & (lane < lo + EPG), logits, neg)
    v1 = jnp.max(el, -1, keepdims=True)
    i1 = jnp.min(jnp.where(el == v1, lane, LANES), -1, keepdims=True)
    el2 = jnp.where(lane == i1, neg, el)
    v2 = jnp.max(el2, -1, keepdims=True)
    i2 = jnp.min(jnp.where(el2 == v2, lane, LANES), -1, keepdims=True)
    e2 = jnp.exp(v2 - v1)
    w1 = p_grp / (1.0 + e2)
    w2 = p_grp * e2 / (1.0 + e2)
    return (jnp.where(lane == 0, w1, jnp.where(lane == 1, w2, 0.0)),
            jnp.where(lane == 0, i1 - N_GROUPS, jnp.where(lane == 1, i2 - N_GROUPS, 0)))


def _post_mixer(o_a, o_b, proj, x, mods, w_br_a, w_br_b, w_o, ln_g, ln_b, w_r, b_r, l, mode, tm,
                tiles_per_batch, alpha):
    n = x.shape[0]
    tok = pl.BlockSpec((tm, D), lambda i: (i, 0))
    lane_out = pl.BlockSpec((tm, LANES), lambda i: (i, 0))
    if proj.ndim == 2:
        mix = tok
        gate_spec = pl.BlockSpec((tm, 2 * D), lambda i: (i, C_MERGE // (2 * D)))
    else:
        t3 = proj.shape[1]
        mix = pl.BlockSpec((tm // t3, t3, D), lambda i: (i, 0, 0))
        gate_spec = pl.BlockSpec((tm // t3, t3, 2 * D), lambda i: (i, 0, C_MERGE // (2 * D)))
    return pl.pallas_call(
        functools.partial(_post_kernel, mode=mode, alpha=alpha, nsub=4),
        grid=(n // tm,),
        in_specs=[
            mix, mix,
            gate_spec,
            tok,
            _mod_spec(mode, l, tm, 2, tiles_per_batch),
            _mod_spec(mode, l, tm, 4, tiles_per_batch),
            _mod_spec(mode, l, tm, 3, tiles_per_batch),
            _layer_spec((D, D), l), _layer_spec((D, D), l), _layer_spec((D, D), l),
            _layer_spec((1, D), l), _layer_spec((1, D), l),
            _layer_spec((D, LANES), l), _layer_spec((1, LANES), l),
        ],
        out_specs=[tok, tok, lane_out, lane_out],
        out_shape=[
            jax.ShapeDtypeStruct((n, D), F32),
            jax.ShapeDtypeStruct((n, D), F32),
            jax.ShapeDtypeStruct((n, LANES), F32),
            jax.ShapeDtypeStruct((n, LANES), I32),
        ],
        compiler_params=_cparams("parallel"),
        name="post_mixer",
    )(o_a, o_b, proj, x, mods, mods, mods, w_br_a, w_br_b, w_o, ln_g, ln_b, w_r, b_r)


TM_EXP = 256


def _dispatch_kernel(d0_ref, d1_ref, h_ref, xs_in, xs_out, sem, *, tm):
    del xs_in

    def copies(r):
        src = h_ref.at[pl.ds(r, 1), :]
        return (pltpu.make_async_copy(src, xs_out.at[pl.ds(d0_ref[r], 1), :], sem),
                pltpu.make_async_copy(src, xs_out.at[pl.ds(d1_ref[r], 1), :], sem))

    def start(r, carry):
        a, b = copies(r)
        a.start()
        b.start()
        return carry

    def wait(r, carry):
        a, b = copies(r)
        a.wait()
        b.wait()
        return carry

    lax.fori_loop(0, tm, start, 0, unroll=4)
    lax.fori_loop(0, tm, wait, 0, unroll=4)


def _dispatch(h2, d0, d1, xs_buf, tm):
    n = h2.shape[0]
    return pl.pallas_call(
        functools.partial(_dispatch_kernel, tm=tm),
        grid=(n // tm,),
        in_specs=[
            pl.BlockSpec((tm,), lambda i: (i,), memory_space=pltpu.SMEM),
            pl.BlockSpec((tm,), lambda i: (i,), memory_space=pltpu.SMEM),
            pl.BlockSpec((tm, D), lambda i: (i, 0)),
            pl.BlockSpec(memory_space=pl.ANY),
        ],
        out_specs=pl.BlockSpec(memory_space=pl.ANY),
        out_shape=jax.ShapeDtypeStruct(xs_buf.shape, F32),
        input_output_aliases={3: 0},
        scratch_shapes=[pltpu.SemaphoreType.DMA],
        compiler_params=_cparams("arbitrary"),
        name="dispatch",
    )(d0, d1, h2, xs_buf)


def _expert_kernel(be_ref, nblk_ref, xs_ref, wg_ref, wu_ref, wd_ref, ys_ref):
    i = pl.program_id(0)

    @pl.when(i < nblk_ref[0])
    def _():
        wg, wu, wd = wg_ref[0, 0].astype(BF16), wu_ref[0, 0].astype(BF16), wd_ref[0, 0].astype(BF16)
        half = TM_EXP // 2
        rows = [slice(0, half), slice(half, TM_EXP)]
        xh = [xs_ref[r, :].astype(BF16) for r in rows]
        gs = [jnp.dot(x, wg, preferred_element_type=F32) for x in xh]
        us = [jnp.dot(x, wu, preferred_element_type=F32) for x in xh]
        hs = [(_silu(g) * u).astype(BF16) for g, u in zip(gs, us)]
        ys = [jnp.dot(h, wd, preferred_element_type=F32) for h in hs]
        for r, y in zip(rows, ys):
            ys_ref[r, :] = y

    @pl.when(i >= nblk_ref[0])
    def _():
        ys_ref[...] = jnp.zeros(ys_ref.shape, F32)


def _experts(xs, block_e, n_used, w_gate, w_up, w_down, l):
    n_rows = xs.shape[0]
    n_blocks = n_rows // TM_EXP
    grid_spec = pltpu.PrefetchScalarGridSpec(
        num_scalar_prefetch=2,
        grid=(n_blocks,),
        in_specs=[
            pl.BlockSpec((TM_EXP, D), lambda i, be, nb: (jnp.minimum(i, nb[0] - 1), 0)),
            pl.BlockSpec((1, 1, D, D_EXP), lambda i, be, nb: (l, be[i], 0, 0)),
            pl.BlockSpec((1, 1, D, D_EXP), lambda i, be, nb: (l, be[i], 0, 0)),
            pl.BlockSpec((1, 1, D_EXP, D), lambda i, be, nb: (l, be[i], 0, 0)),
        ],
        out_specs=pl.BlockSpec((TM_EXP, D), lambda i, be, nb: (i, 0)),
    )
    return pl.pallas_call(
        _expert_kernel,
        grid_spec=grid_spec,
        out_shape=jax.ShapeDtypeStruct((n_rows, D), F32),
        compiler_params=_cparams("arbitrary"),
        name="experts",
    )(block_e, n_used, xs, w_gate, w_up, w_down)


def _combine_kernel(d0_ref, d1_ref, ys_hbm, rw_ref, x_ref, g2_ref, lg_ref, lb_ref, o_ref,
                    y0, y1, sem, *, mode, alpha, tm):
    def copies(r):
        return (pltpu.make_async_copy(ys_hbm.at[pl.ds(d0_ref[r], 1), :], y0.at[pl.ds(r, 1), :], sem),
                pltpu.make_async_copy(ys_hbm.at[pl.ds(d1_ref[r], 1), :], y1.at[pl.ds(r, 1), :], sem))

    def start(r, carry):
        a, b = copies(r)
        a.start()
        b.start()
        return carry

    def wait(r, carry):
        a, b = copies(r)
        a.wait()
        b.wait()
        return carry

    lax.fori_loop(0, tm, start, 0, unroll=4)
    lax.fori_loop(0, tm, wait, 0, unroll=4)
    rw = rw_ref[...]
    y = rw[:, 0:1] * y0[...] + rw[:, 1:2] * y1[...]
    o_ref[...] = _layer_norm(alpha * x_ref[...] + (1.0 + _mod_val(g2_ref, mode)) * y,
                             lg_ref[0], lb_ref[0])


def _combine(ys, d0, d1, rw, x1, mods, ln_g, ln_b, l, mode, tm, tiles_per_batch, alpha):
    n = x1.shape[0]
    return pl.pallas_call(
        functools.partial(_combine_kernel, mode=mode, alpha=alpha, tm=tm),
        grid=(n // tm,),
        in_specs=[
            pl.BlockSpec((tm,), lambda i: (i,), memory_space=pltpu.SMEM),
            pl.BlockSpec((tm,), lambda i: (i,), memory_space=pltpu.SMEM),
            pl.BlockSpec(memory_space=pl.ANY),
            pl.BlockSpec((tm, LANES), lambda i: (i, 0)),
            pl.BlockSpec((tm, D), lambda i: (i, 0)),
            _mod_spec(mode, l, tm, 5, tiles_per_batch),
            _layer_spec((1, D), l),
            _layer_spec((1, D), l),
        ],
        out_specs=pl.BlockSpec((tm, D), lambda i: (i, 0)),
        out_shape=jax.ShapeDtypeStruct((n, D), F32),
        scratch_shapes=[pltpu.VMEM((tm, D), F32), pltpu.VMEM((tm, D), F32), pltpu.SemaphoreType.DMA],
        compiler_params=_cparams("arbitrary"),
        name="combine",
    )(d0, d1, ys, rw, x1, mods, ln_g, ln_b)


def _route_meta(eid):
    n = eid.shape[0]
    flat = eid.reshape(-1)
    n_slots = flat.shape[0]
    onehot = (flat[:, None] == jnp.arange(N_EXP, dtype=I32)[None, :]).astype(I32)
    csum = jnp.cumsum(onehot, axis=0)
    counts = csum[-1]
    rank = jnp.take_along_axis(csum - onehot, flat[:, None], axis=1)[:, 0]
    padded = (counts + TM_EXP - 1) // TM_EXP * TM_EXP
    pad_end = jnp.cumsum(padded)
    pad_start = pad_end - padded
    dest = pad_start[flat] + rank
    n_blocks = (n_slots + N_EXP * (TM_EXP - 1)) // TM_EXP
    block_start = jnp.arange(n_blocks, dtype=I32) * TM_EXP
    block_e = jnp.minimum(jnp.sum(pad_end[None, :] <= block_start[:, None], axis=1), N_EXP - 1).astype(I32)
    n_used = (pad_end[-1:] // TM_EXP).astype(I32)
    dest = dest.reshape(n, 2)
    return n_blocks * TM_EXP, block_e, n_used, dest[:, 0], dest[:, 1]


def _pack_w_in(w_in):
    off_g_a = CONV_DIM
    off_decay = off_g_a + D
    off_q_b = off_decay + 2 * NH
    off_merge = off_q_b + 4 * D
    depth = w_in.shape[0]
    pad = jnp.zeros((depth, D, LANES - 2 * NH), w_in.dtype)
    small = jnp.concatenate([w_in[:, :, off_decay:off_q_b], pad], axis=2)
    main = jnp.concatenate([w_in[:, :, :off_decay], w_in[:, :, off_q_b:off_merge + 2 * D]], axis=2)
    return main.astype(BF16), small.astype(BF16)


def _pad_lanes(v):
    depth, k = v.shape
    return jnp.concatenate([v, jnp.zeros((depth, LANES - k), v.dtype)], axis=1).reshape(depth, 1, LANES)


def kernel(x_prompt, x_sample, state_conv, state_delta, state_hgrn, c_prompt, c_sample, ln0_g, ln0_b, w_ada, b_ada, w_in, conv_w, a_log, dt_bias, norm_a, lb_param, norm_b, w_br_a, w_br_b, w_o, ln1_g, ln1_b, w_grp, b_grp, w_rt, b_rt, w_gate_e, w_up_e, w_down_e, ln2_g, ln2_b):
    depth = w_in.shape[0]
    alpha = (2 * depth) ** 0.25
    bp, tp, _ = x_prompt.shape
    bs, ts, _ = x_sample.shape
    n_p = bp * tp
    tsp = SUBLANES
    assert ts <= tsp
    n_s = bs * tsp

    w_main, w_small = _pack_w_in(w_in)
    w_a16, w_b16, w_o16 = w_br_a.astype(BF16), w_br_b.astype(BF16), w_o.astype(BF16)
    w_r = jnp.concatenate([w_grp, w_rt, jnp.zeros((depth, D, LANES - N_GROUPS - N_EXP), F32)], axis=2)
    b_r = jnp.concatenate([b_grp, b_rt, jnp.zeros((depth, LANES - N_GROUPS - N_EXP), F32)], axis=1)
    b_r = b_r.reshape(depth, 1, LANES)
    a_log_p, dt_bias_p = _pad_lanes(a_log), _pad_lanes(dt_bias)
    norm_a3, norm_b3 = norm_a.reshape(depth, 1, HD), norm_b.reshape(depth, 1, HD)
    ln1_g3, ln1_b3 = ln1_g.reshape(depth, 1, D), ln1_b.reshape(depth, 1, D)
    ln2_g3, ln2_b3 = ln2_g.reshape(depth, 1, D), ln2_b.reshape(depth, 1, D)

    mods = _modulation(jnp.concatenate([c_prompt, c_sample], axis=0), w_ada, b_ada)
    mods_p = mods[:, :bp].reshape(depth, bp, 1, 6 * D)
    mods_s = jnp.repeat(mods[:, bp:], tsp, axis=1)

    xs_pad = jnp.concatenate([x_sample, jnp.zeros((bs, tsp - ts, D), x_sample.dtype)], axis=1)
    cp = math.gcd(tp, 128)
    tm_p = math.gcd(tp, 512)
    tm_s = math.gcd(n_s, 512)
    bb_s = math.gcd(bs, 4)
    state_shape = lambda nb: (depth, nb, NH, HD, HD)
    conv_shape = lambda nb: (depth, nb, CONV_W - 1, CONV_DIM)
    groups = [
        dict(mode="row", n=n_p, nb=bp, t=tp, C=cp, tv=cp, tm=tm_p, tpb=tp // tm_p, bb=1, mods=mods_p,
             conv=jnp.zeros(conv_shape(bp), F32), delta=jnp.zeros(state_shape(bp), F32),
             hgrn=jnp.zeros(state_shape(bp), F32)),
        dict(mode="tok", n=n_s, nb=bs, t=tsp, C=tsp, tv=ts, tm=tm_s, tpb=1, bb=bb_s, mods=mods_s,
             conv=state_conv, delta=state_delta, hgrn=state_hgrn),
    ]
    for g in groups:
        g["tm_dma"] = math.gcd(g["t"] if g["mode"] == "row" else g["n"], 1024)
    for g in groups:
        g["conv_o"] = jnp.zeros(conv_shape(g["nb"]), F32)
        g["delta_o"] = jnp.zeros(state_shape(g["nb"]), F32)
        g["hgrn_o"] = jnp.zeros(state_shape(g["nb"]), F32)
    xs = [_ln0(x_prompt.reshape(n_p, D), ln0_g, ln0_b, tm_p),
          _ln0(xs_pad.reshape(n_s, D), ln0_g, ln0_b, tm_s)]

    for l in range(depth):
        x1s, h2s, rws, res = [], [], [], []
        for gi, g in enumerate(groups):
            tm_in = math.gcd(g["t"] if g["mode"] == "row" else g["n"], 1024)
            if g["mode"] == "row":
                proj, small = _inproj(xs[gi], g["mods"], w_main, w_small, l, g["mode"], tm_in,
                                      g["t"] // tm_in)
                proj3 = proj.reshape(g["nb"], g["t"], NPROJ)
                small3 = small.reshape(g["nb"], g["t"], LANES)
            else:
                proj3, small3 = _inproj(xs[gi], g["mods"], w_main, w_small, l, g["mode"], tm_in, 1,
                                        t3=g["t"])
                proj = proj3
            o_a, g["conv_o"], g["delta_o"] = _mixer_a(
                proj3, small3, g["conv"], g["delta"], g["conv_o"], g["delta_o"], conv_w, a_log_p, dt_bias_p,
                norm_a3, l, g["C"], g["tv"], g["bb"])
            o_b, g["hgrn_o"] = _mixer_b(proj3, g["hgrn"], g["hgrn_o"], lb_param, norm_b3, l, g["C"],
                                        g["tv"], g["bb"])
            halve = 1 if g["mode"] == "row" else 2
            if g["mode"] == "row":
                o_a, o_b = o_a.reshape(g["n"], D), o_b.reshape(g["n"], D)
            x1, h2, rw, re = _post_mixer(o_a, o_b, proj, xs[gi],
                                         g["mods"], w_a16, w_b16, w_o16, ln1_g3, ln1_b3, w_r, b_r, l,
                                         g["mode"], g["tm"] // halve, g["tpb"] * halve, alpha)
            x1s.append(x1)
            h2s.append(h2)
            rws.append(rw)
            res.append(re)
        eid = jnp.concatenate([r[:, :2] for r in res], axis=0)
        n_rows, block_e, n_used, d0, d1 = _route_meta(eid)
        if l == 0:
            xs_sorted = jnp.zeros((n_rows, D), F32)
        off = 0
        for gi, g in enumerate(groups):
            sl = slice(off, off + g["n"])
            xs_sorted = _dispatch(h2s[gi], d0[sl], d1[sl], xs_sorted, g["tm_dma"])
            off += g["n"]
        ys = _experts(xs_sorted, block_e, n_used, w_gate_e, w_up_e, w_down_e, l)
        off = 0
        for gi, g in enumerate(groups):
            sl = slice(off, off + g["n"])
            xs[gi] = _combine(ys, d0[sl], d1[sl], rws[gi], x1s[gi], g["mods"], ln2_g3, ln2_b3, l,
                              g["mode"], g["tm_dma"], g["t"] // g["tm_dma"], alpha)
            off += g["n"]

    y_prompt = xs[0].reshape(bp, tp, D)
    y_sample = xs[1].reshape(bs, tsp, D)[:, :ts]
    gp, gs = groups
    return (y_prompt, y_sample, gp["conv_o"], gp["delta_o"], gp["hgrn_o"],
            gs["conv_o"], gs["delta_o"], gs["hgrn_o"])
```

```python
import functools
import math

import jax
import jax.numpy as jnp
from jax import lax
from jax.experimental import pallas as pl
from jax.experimental.pallas import tpu as pltpu

F32 = jnp.float32
BF16 = jnp.bfloat16
I32 = jnp.int32

D = 1024
NH = 8
HD = 128
CONV_W = 4
CONV_DIM = 3 * D
N_GROUPS = 4
EPG = 8
N_EXP = N_GROUPS * EPG
D_EXP = 512
F_MIN = 1e-30
LN_EPS = 1e-5
RMS_EPS = 1e-6

C_QKV = 0
C_GA = 3 * D
C_QB = 4 * D
C_FB = 5 * D
C_IB = 6 * D
C_GB = 7 * D
C_MERGE = 8 * D
NPROJ = 10 * D

LANES = 128
SUBLANES = 8
VMEM_LIMIT = 48 * 1024 * 1024

HI = lax.Precision.HIGHEST


def _cparams(*sem):
    return pltpu.CompilerParams(dimension_semantics=sem, vmem_limit_bytes=VMEM_LIMIT)


def _dot(a, b, hi=False):
    if hi:
        return jnp.dot(a, b, preferred_element_type=F32, precision=HI)
    return jnp.dot(a.astype(BF16), b.astype(BF16), preferred_element_type=F32)


def _dot_nt(a, b):
    return lax.dot_general(a.astype(BF16), b.astype(BF16), (((1,), (1,)), ((), ())),
                           preferred_element_type=F32)


def _dot_tn(a, b):
    return lax.dot_general(a.astype(BF16), b.astype(BF16), (((0,), (0,)), ((), ())),
                           preferred_element_type=F32)


def _split(a):
    hi = a.astype(BF16)
    return hi, (a - hi.astype(F32)).astype(BF16)


def _dot3(a, b):
    mm = lambda x, y: jnp.dot(x, y, preferred_element_type=F32)
    return mm(a[0], b[0]) + (mm(a[0], b[1]) + mm(a[1], b[0]))


def _sigmoid(x):
    return 1.0 / (1.0 + jnp.exp(-x))


def _silu(x):
    return x * _sigmoid(x)


def _softplus(x):
    return jnp.maximum(x, 0.0) + jnp.log(1.0 + jnp.exp(-jnp.abs(x)))


def _layer_norm(x, g, b):
    mu = jnp.mean(x, -1, keepdims=True)
    xc = x - mu
    var = jnp.mean(xc * xc, -1, keepdims=True)
    return xc * lax.rsqrt(var + LN_EPS) * g + b


def _layer_spec(shape, l):
    zeros = (0,) * len(shape)
    return pl.BlockSpec((1,) + tuple(shape), lambda *_: (l,) + zeros)


def _mod_kernel(c_ref, w_ref, b_ref, o_ref):
    ca = _silu(c_ref[...])
    o_ref[0] = _dot(ca, w_ref[0]) + b_ref[0]


def _modulation(c_all, w_ada, b_ada):
    depth = w_ada.shape[0]
    nb = c_all.shape[0]
    tn = 1536
    return pl.pallas_call(
        _mod_kernel,
        grid=(depth, 6 * D // tn),
        in_specs=[
            pl.BlockSpec((nb, D), lambda l, j: (0, 0)),
            pl.BlockSpec((1, D, tn), lambda l, j: (l, 0, j)),
            pl.BlockSpec((1, 1, tn), lambda l, j: (l, 0, j)),
        ],
        out_specs=pl.BlockSpec((1, nb, tn), lambda l, j: (l, 0, j)),
        out_shape=jax.ShapeDtypeStruct((depth, nb, 6 * D), F32),
        compiler_params=_cparams("parallel", "parallel"),
        name="adaln_mod",
    )(c_all, w_ada, b_ada.reshape(depth, 1, 6 * D))


def _mod_spec(mode, l, tm, k, tiles_per_batch):
    if mode == "row":
        return pl.BlockSpec((1, 1, 1, D), lambda i, *_: (l, i // tiles_per_batch, 0, k))
    return pl.BlockSpec((1, tm, D), lambda i, *_: (l, i, k))


def _mod_val(ref, mode):
    return ref[0, 0] if mode == "row" else ref[0]


def _ln0_kernel(x_ref, g_ref, b_ref, o_ref):
    o_ref[...] = _layer_norm(x_ref[...], g_ref[...], b_ref[...])


def _ln0(x, g, b, tm):
    n = x.shape[0]
    return pl.pallas_call(
        _ln0_kernel,
        grid=(n // tm,),
        in_specs=[pl.BlockSpec((tm, D), lambda i: (i, 0)),
                  pl.BlockSpec((1, D), lambda i: (0, 0)),
                  pl.BlockSpec((1, D), lambda i: (0, 0))],
        out_specs=pl.BlockSpec((tm, D), lambda i: (i, 0)),
        out_shape=jax.ShapeDtypeStruct((n, D), F32),
        compiler_params=_cparams("parallel"),
        name="ln0",
    )(x, g.reshape(1, D), b.reshape(1, D))


def _inproj_kernel(x_ref, sc_ref, sh_ref, w_ref, ws_ref, o_ref, os_ref, h_scr, *, mode):
    @pl.when(pl.program_id(1) == 0)
    def _():
        h = x_ref[...] * (1.0 + _mod_val(sc_ref, mode)) + _mod_val(sh_ref, mode)
        h_scr[...] = h.astype(BF16)
        os_ref[...] = jnp.dot(h_scr[...], ws_ref[0], preferred_element_type=F32).reshape(os_ref.shape)

    o_ref[...] = jnp.dot(h_scr[...], w_ref[0], preferred_element_type=F32).reshape(o_ref.shape)


def _inproj(x, mods, w_main, w_small, l, mode, tm, tiles_per_batch, t3=None):
    n = x.shape[0]
    tn = 1280
    if t3 is None:
        out_specs = [pl.BlockSpec((tm, tn), lambda i, j: (i, j)),
                     pl.BlockSpec((tm, LANES), lambda i, j: (i, 0))]
        out_shape = [jax.ShapeDtypeStruct((n, NPROJ), F32), jax.ShapeDtypeStruct((n, LANES), F32)]
    else:
        out_specs = [pl.BlockSpec((tm // t3, t3, tn), lambda i, j: (i, 0, j)),
                     pl.BlockSpec((tm // t3, t3, LANES), lambda i, j: (i, 0, 0))]
        out_shape = [jax.ShapeDtypeStruct((n // t3, t3, NPROJ), F32),
                     jax.ShapeDtypeStruct((n // t3, t3, LANES), F32)]
    return pl.pallas_call(
        functools.partial(_inproj_kernel, mode=mode),
        grid=(n // tm, NPROJ // tn),
        in_specs=[
            pl.BlockSpec((tm, D), lambda i, j: (i, 0)),
            _mod_spec(mode, l, tm, 1, tiles_per_batch),
            _mod_spec(mode, l, tm, 0, tiles_per_batch),
            pl.BlockSpec((1, D, tn), lambda i, j: (l, 0, j)),
            pl.BlockSpec((1, D, LANES), lambda i, j: (l, 0, 0)),
        ],
        out_specs=out_specs,
        out_shape=out_shape,
        scratch_shapes=[pltpu.VMEM((tm, D), BF16)],
        compiler_params=_cparams("parallel", "arbitrary"),
        name="inproj",
    )(x, mods, mods, w_main, w_small)


def _mixa_kernel(qkv_ref, ga_ref, sm_ref, cst_ref, s0_ref, cw_ref, alog_ref, dtb_ref, na_ref,
                 cbuf_ref, sbuf_ref, oa_ref, cso_ref, so_ref, xbuf, s_scr, *, C, tv, nc, bb):
    del cbuf_ref, sbuf_ref
    c = pl.program_id(1)
    row = lax.broadcasted_iota(I32, (C, 1), 0)
    ri = lax.broadcasted_iota(I32, (C, C), 0)
    ci = lax.broadcasted_iota(I32, (C, C), 1)
    causal = ri >= ci
    strict = ri > ci
    eye = (ri == ci).astype(F32)
    tril = causal.astype(F32)
    hist = SUBLANES - (CONV_W - 1)

    @pl.when(c == 0)
    def _():
        for bi in range(bb):
            xbuf[bi, 0:SUBLANES, :] = jnp.zeros((SUBLANES, CONV_DIM), F32)
            xbuf[bi, hist:SUBLANES, :] = cst_ref[0, bi]
            s_scr[bi] = s0_ref[0, bi]

    pre = []
    for bi in range(bb):
        xbuf[bi, SUBLANES:SUBLANES + C, :] = qkv_ref[bi]
        sm = sm_ref[bi]
        g_all = -jnp.exp(alog_ref[0]) * _softplus(sm + dtb_ref[0])
        beta_all = _sigmoid(sm)
        if tv < C:
            g_all = jnp.where(row < tv, g_all, 0.0)
            beta_all = jnp.where(row < tv, beta_all, 0.0)
        gcum_all = _dot(tril, g_all, hi=True)
        pre.append((gcum_all, gcum_all.T, jnp.exp(gcum_all), beta_all))

    n_nil = min(C, tv)
    n_double = max(int(math.ceil(math.log2(n_nil))) - 1, 0)
    units = [(bi, h) for bi in range(bb) for h in range(NH)]

    st = []
    for bi, h in units:
        gcum_all, gcum_t, eg_all, beta_all = pre[bi]
        conv = []
        for part in range(3):
            col = slice(part * D + h * HD, part * D + (h + 1) * HD)
            acc = None
            for j in range(CONV_W):
                term = xbuf[bi, hist + j:hist + j + C, col] * cw_ref[0, j:j + 1, col]
                acc = term if acc is None else acc + term
            conv.append(_silu(acc))
        q, k, v = conv
        q = q * lax.rsqrt(jnp.sum(q * q, -1, keepdims=True) + RMS_EPS) * (HD ** -0.5)
        k = k * lax.rsqrt(jnp.sum(k * k, -1, keepdims=True) + RMS_EPS)
        gc = gcum_all[:, h:h + 1]
        gr = gcum_t[h:h + 1, :]
        bt = beta_all[:, NH + h:NH + h + 1]
        eg = eg_all[:, h:h + 1]
        glast = gcum_all[tv - 1:tv, h:h + 1]
        dec = jnp.where(causal, jnp.exp(jnp.where(causal, gc - gr, 0.0)), 0.0)
        kb = k * bt
        kk = _dot_nt(jnp.concatenate([kb, q], axis=0), k)
        st.append(dict(
            a=-jnp.where(strict, kk[:C] * dec, 0.0),
            rhs=jnp.concatenate([kb * eg, v * bt], axis=1),
            qk=jnp.where(causal, kk[C:] * dec, 0.0),
            qe=q * eg,
            kg=k * jnp.exp(glast - gc),
            eglast=jnp.exp(glast),
        ))

    ps = [eye + u["a"] for u in st]
    qsp = [_split(u["a"]) for u in st]
    for _ in range(n_double):
        qs = [_dot3(s2, s2) for s2 in qsp]
        qsp = [_split(x) for x in qs]
        ps = [p + _dot3(_split(p), s2) for p, s2 in zip(ps, qsp)]
    sols = [_dot3(_split(p), _split(u["rhs"])) for p, u in zip(ps, st)]

    ss = [s_scr[bi, h] for bi, h in units]
    gates = [_silu(ga_ref[bi, :, h * HD:(h + 1) * HD]) for bi, h in units]
    wss = [_dot(jnp.concatenate([sol[:, :HD], u["qe"]], axis=0), s)
           for u, sol, s in zip(st, sols, ss)]
    vns = [sol[:, HD:] - ws[:C] for sol, ws in zip(sols, wss)]
    os_ = [ws[C:] + _dot(u["qk"], vn) for u, ws, vn in zip(st, wss, vns)]
    s_new = [u["eglast"] * s + _dot_tn(u["kg"], vn) for u, s, vn in zip(st, ss, vns)]
    ys = [o * lax.rsqrt(jnp.mean(o * o, -1, keepdims=True) + RMS_EPS) * na_ref[0] * gt
          for o, gt in zip(os_, gates)]
    for (bi, h), sn, y in zip(units, s_new, ys):
        s_scr[bi, h] = sn
        oa_ref[bi, :, h * HD:(h + 1) * HD] = y

    if nc > 1:
        for bi in range(bb):
            xbuf[bi, 0:SUBLANES, :] = xbuf[bi, C:C + SUBLANES, :]

    @pl.when(c == nc - 1)
    def _():
        for bi in range(bb):
            if nc > 1:
                cso_ref[0, bi] = xbuf[bi, hist:SUBLANES, :]
            else:
                cso_ref[0, bi] = xbuf[bi, SUBLANES + tv - (CONV_W - 1):SUBLANES + tv, :]
            so_ref[0, bi] = s_scr[bi]


def _mixer_a(proj3, small3, conv_state, s0, conv_buf, s_buf, conv_w, a_log_p, dt_bias_p, norm_a, l, C, tv,
             bb):
    nb, t, _ = proj3.shape
    nc = t // C
    assert nc == 1 or tv == C
    return pl.pallas_call(
        functools.partial(_mixa_kernel, C=C, tv=tv, nc=nc, bb=bb),
        grid=(nb // bb, nc),
        in_specs=[
            pl.BlockSpec((bb, C, CONV_DIM), lambda b, c: (b, c, C_QKV // CONV_DIM)),
            pl.BlockSpec((bb, C, D), lambda b, c: (b, c, C_GA // D)),
            pl.BlockSpec((bb, C, LANES), lambda b, c: (b, c, 0)),
            pl.BlockSpec((1, bb, CONV_W - 1, CONV_DIM), lambda b, c: (l, b, 0, 0)),
            pl.BlockSpec((1, bb, NH, HD, HD), lambda b, c: (l, b, 0, 0, 0)),
            _layer_spec((CONV_W, CONV_DIM), l),
            _layer_spec((1, LANES), l),
            _layer_spec((1, LANES), l),
            _layer_spec((1, HD), l),
            pl.BlockSpec(memory_space=pl.ANY),
            pl.BlockSpec(memory_space=pl.ANY),
        ],
        out_specs=[
            pl.BlockSpec((bb, C, D), lambda b, c: (b, c, 0)),
            pl.BlockSpec((1, bb, CONV_W - 1, CONV_DIM), lambda b, c: (l, b, 0, 0)),
            pl.BlockSpec((1, bb, NH, HD, HD), lambda b, c: (l, b, 0, 0, 0)),
        ],
        out_shape=[
            jax.ShapeDtypeStruct((nb, t, D), F32),
            jax.ShapeDtypeStruct(conv_buf.shape, F32),
            jax.ShapeDtypeStruct(s_buf.shape, F32),
        ],
        input_output_aliases={9: 1, 10: 2},
        scratch_shapes=[pltpu.VMEM((bb, C + 2 * SUBLANES, CONV_DIM), F32),
                        pltpu.VMEM((bb, NH, HD, HD), F32)],
        compiler_params=_cparams("parallel", "arbitrary"),
        name="mixer_a",
    )(proj3, proj3, small3, conv_state, s0, conv_w, a_log_p, dt_bias_p, norm_a, conv_buf, s_buf)


DIAG = SUBLANES


def _mixb_kernel(qb_ref, fb_ref, ib_ref, gb_ref, s0_ref, lbp_ref, nb_ref, sbuf_ref,
                 ob_ref, so_ref, st_scr, *, C, tv, nc, bb, layer):
    del sbuf_ref
    c = pl.program_id(1)
    row = lax.broadcasted_iota(I32, (C, 1), 0)
    ri = lax.broadcasted_iota(I32, (C, C), 0)
    ci = lax.broadcasted_iota(I32, (C, C), 1)
    tril = (ri >= ci).astype(F32)

    @pl.when(c == 0)
    def _():
        for bi in range(bb):
            for h in range(NH):
                st_scr[bi, h] = s0_ref[0, bi, h].T

    lbp = lbp_ref[...]
    e = jnp.exp(lbp - jnp.max(lbp, axis=0, keepdims=True))
    psm = e / jnp.sum(e, axis=0, keepdims=True)
    lb = jnp.zeros((1, D), F32)
    for j in range(1, layer + 1):
        lb = lb + psm[j:j + 1, :]

    row8 = lax.broadcasted_iota(I32, (DIAG, 1), 0)
    levels = []
    lv = DIAG
    while lv < C:
        levels.append(lv)
        lv *= 2

    units = []
    for bi in range(bb):
        z = fb_ref[bi]
        f = lb + (1.0 - lb) * _sigmoid(z)
        lf_all = jnp.log(jnp.maximum(f, F_MIN))
        kk_all = (1.0 - lb) * _sigmoid(-z)
        if tv < C:
            lf_all = jnp.where(row < tv, lf_all, 0.0)
            kk_all = jnp.where(row < tv, kk_all, 0.0)
        b_all = _dot(tril, lf_all, hi=True)

        for h in range(NH):
            cs = slice(h * HD, (h + 1) * HD)
            q = _silu(qb_ref[bi, :, cs])
            kk = kk_all[:, cs]
            iv = ib_ref[bi, :, cs]
            b = b_all[:, cs]
            blast = b[tv - 1:tv, :]

            amat = None
            for lv in levels:
                pieces = []
                for j in range(C // lv):
                    blk = b[j * lv:(j + 1) * lv, :]
                    if j % 2 == 1:
                        pieces.append(blk - b[j * lv:j * lv + 1, :])
                    else:
                        pieces.append(b[(j + 1) * lv:(j + 1) * lv + 1, :] - blk)
                ex = jnp.exp(jnp.concatenate(pieces, axis=0))
                lower = ((row // lv) % 2) == 1
                qs = jnp.where(lower, q * ex, 0.0)
                ks = jnp.where(lower, 0.0, kk * ex)
                blk_a = jnp.where((ri // (2 * lv)) == (ci // (2 * lv)), _dot_nt(qs, ks), 0.0)
                amat = blk_a if amat is None else amat + blk_a

            od = []
            for j in range(C // DIAG):
                rs = slice(j * DIAG, (j + 1) * DIAG)
                bd, qq, k8, i8 = b[rs, :], q[rs, :], kk[rs, :], iv[rs, :]
                acc = jnp.zeros((DIAG, HD), F32)
                for s in range(DIAG):
                    ex = jnp.exp(jnp.where(row8 >= s, bd - bd[s:s + 1, :], -jnp.inf))
                    zz = qq * ex * k8[s:s + 1, :]
                    acc = acc + jnp.sum(zz, -1, keepdims=True) * i8[s:s + 1, :]
                od.append(acc)
            o = od[0] if len(od) == 1 else jnp.concatenate(od, axis=0)
            if amat is not None:
                o = o + _dot(amat, iv)

            units.append(dict(bi=bi, h=h, o=o, qe=q * jnp.exp(b), kg=kk * jnp.exp(blast - b), iv=iv,
                              eb=jnp.exp(blast), gate=_silu(gb_ref[bi, :, cs])))

    sts = [st_scr[u["bi"], u["h"]] for u in units]
    os_ = [u["o"] + _dot_nt(u["qe"], st) for u, st in zip(units, sts)]
    st_new = [st * u["eb"] + _dot_tn(u["iv"], u["kg"]) for u, st in zip(units, sts)]
    ys = [o * lax.rsqrt(jnp.mean(o * o, -1, keepdims=True) + RMS_EPS) * nb_ref[0] * u["gate"]
          for u, o in zip(units, os_)]
    for u, sn, y in zip(units, st_new, ys):
        st_scr[u["bi"], u["h"]] = sn
        ob_ref[u["bi"], :, u["h"] * HD:(u["h"] + 1) * HD] = y

    @pl.when(c == nc - 1)
    def _():
        for bi in range(bb):
            for h in range(NH):
                so_ref[0, bi, h] = st_scr[bi, h].T


def _mixer_b(proj3, s0, s_buf, lb_param, norm_b, l, C, tv, bb):
    nb, t, _ = proj3.shape
    nc = t // C
    depth = lb_param.shape[0]
    tok = lambda col: pl.BlockSpec((bb, C, D), lambda b, c: (b, c, col // D))
    return pl.pallas_call(
        functools.partial(_mixb_kernel, C=C, tv=tv, nc=nc, bb=bb, layer=l),
        grid=(nb // bb, nc),
        in_specs=[
            tok(C_QB), tok(C_FB), tok(C_IB), tok(C_GB),
            pl.BlockSpec((1, bb, NH, HD, HD), lambda b, c: (l, b, 0, 0, 0)),
            pl.BlockSpec((depth, D), lambda b, c: (0, 0)),
            _layer_spec((1, HD), l),
            pl.BlockSpec(memory_space=pl.ANY),
        ],
        out_specs=[
            pl.BlockSpec((bb, C, D), lambda b, c: (b, c, 0)),
            pl.BlockSpec((1, bb, NH, HD, HD), lambda b, c: (l, b, 0, 0, 0)),
        ],
        out_shape=[
            jax.ShapeDtypeStruct((nb, t, D), F32),
            jax.ShapeDtypeStruct(s_buf.shape, F32),
        ],
        input_output_aliases={7: 1},
        scratch_shapes=[pltpu.VMEM((bb, NH, HD, HD), F32)],
        compiler_params=_cparams("parallel", "arbitrary"),
        name="mixer_b",
    )(proj3, proj3, proj3, proj3, s0, lb_param, norm_b, s_buf)


def _post_kernel(oa_ref, ob_ref, gt_ref, x_ref, g1_ref, sc2_ref, sh2_ref, wa_ref, wb_ref, wo_ref,
                 lg_ref, lb_ref, wr_ref, br_ref, x1_ref, h2_ref, rw_ref, re_ref, *, mode, alpha, nsub):
    sub = x_ref.shape[0] // nsub
    rows = [slice(k * sub, (k + 1) * sub) for k in range(nsub)]
    mod = lambda ref, r: ref[0, 0] if mode == "row" else ref[0, r, :]

    def tok(ref, k):
        if len(ref.shape) == 2:
            return ref[rows[k], :]
        t3 = ref.shape[1]
        return ref[k * sub // t3:(k + 1) * sub // t3].reshape(sub, ref.shape[2])

    gates = [_sigmoid(tok(gt_ref, k)) for k in range(nsub)]
    pa = [_dot(tok(oa_ref, k), wa_ref[0]) for k in range(nsub)]
    pb = [_dot(tok(ob_ref, k), wb_ref[0]) for k in range(nsub)]
    ys = [_dot(g[:, :D] * a + g[:, D:] * b, wo_ref[0]) for g, a, b in zip(gates, pa, pb)]
    x1s = [_layer_norm(alpha * x_ref[r, :] + (1.0 + mod(g1_ref, r)) * y, lg_ref[0], lb_ref[0])
           for r, y in zip(rows, ys)]
    h2s = [x1 * (1.0 + mod(sc2_ref, r)) + mod(sh2_ref, r) for r, x1 in zip(rows, x1s)]
    lgs = [_dot(h2, wr_ref[0], hi=True) + br_ref[0] for h2 in h2s]
    for r, x1, h2, logits in zip(rows, x1s, h2s, lgs):
        x1_ref[r, :] = x1
        h2_ref[r, :] = h2
        rw, re = _route(logits)
        rw_ref[r, :] = rw
        re_ref[r, :] = re


def _route(logits):
    lane = lax.broadcasted_iota(I32, logits.shape, 1)
    neg = jnp.float32(-jnp.inf)
    gl = jnp.where(lane < N_GROUPS, logits, neg)
    gmax = jnp.max(gl, -1, keepdims=True)
    grp = jnp.min(jnp.where(gl == gmax, lane, LANES), -1, keepdims=True)
    p_grp = 1.0 / jnp.sum(jnp.exp(gl - gmax), -1, keepdims=True)
    lo = N_GROUPS + grp * EPG
    el = jnp.where((lane >= lo) & (lane < lo + EPG), logits, neg)
    v1 = jnp.max(el, -1, keepdims=True)
    i1 = jnp.min(jnp.where(el == v1, lane, LANES), -1, keepdims=True)
    el2 = jnp.where(lane == i1, neg, el)
    v2 = jnp.max(el2, -1, keepdims=True)
    i2 = jnp.min(jnp.where(el2 == v2, lane, LANES), -1, keepdims=True)
    e2 = jnp.exp(v2 - v1)
    w1 = p_grp / (1.0 + e2)
    w2 = p_grp * e2 / (1.0 + e2)
    return (jnp.where(lane == 0, w1, jnp.where(lane == 1, w2, 0.0)),
            jnp.where(lane == 0, i1 - N_GROUPS, jnp.where(lane == 1, i2 - N_GROUPS, 0)))


def _post_mixer(o_a, o_b, proj, x, mods, w_br_a, w_br_b, w_o, ln_g, ln_b, w_r, b_r, l, mode, tm,
                tiles_per_batch, alpha):
    n = x.shape[0]
    tok = pl.BlockSpec((tm, D), lambda i: (i, 0))
    lane_out = pl.BlockSpec((tm, LANES), lambda i: (i, 0))
    if proj.ndim == 2:
        mix = tok
        gate_spec = pl.BlockSpec((tm, 2 * D), lambda i: (i, C_MERGE // (2 * D)))
    else:
        t3 = proj.shape[1]
        mix = pl.BlockSpec((tm // t3, t3, D), lambda i: (i, 0, 0))
        gate_spec = pl.BlockSpec((tm // t3, t3, 2 * D), lambda i: (i, 0, C_MERGE // (2 * D)))
    return pl.pallas_call(
        functools.partial(_post_kernel, mode=mode, alpha=alpha, nsub=4),
        grid=(n // tm,),
        in_specs=[
            mix, mix,
            gate_spec,
            tok,
            _mod_spec(mode, l, tm, 2, tiles_per_batch),
            _mod_spec(mode, l, tm, 4, tiles_per_batch),
            _mod_spec(mode, l, tm, 3, tiles_per_batch),
            _layer_spec((D, D), l), _layer_spec((D, D), l), _layer_spec((D, D), l),
            _layer_spec((1, D), l), _layer_spec((1, D), l),
            _layer_spec((D, LANES), l), _layer_spec((1, LANES), l),
        ],
        out_specs=[tok, tok, lane_out, lane_out],
        out_shape=[
            jax.ShapeDtypeStruct((n, D), F32),
            jax.ShapeDtypeStruct((n, D), F32),
            jax.ShapeDtypeStruct((n, LANES), F32),
            jax.ShapeDtypeStruct((n, LANES), I32),
        ],
        compiler_params=_cparams("parallel"),
        name="post_mixer",
    )(o_a, o_b, proj, x, mods, mods, mods, w_br_a, w_br_b, w_o, ln_g, ln_b, w_r, b_r)


TM_EXP = 256


def _dispatch_kernel(d0_ref, d1_ref, h_ref, xs_in, xs_out, sem, *, tm):
    del xs_in

    def copies(r):
        src = h_ref.at[pl.ds(r, 1), :]
        return (pltpu.make_async_copy(src, xs_out.at[pl.ds(d0_ref[r], 1), :], sem),
                pltpu.make_async_copy(src, xs_out.at[pl.ds(d1_ref[r], 1), :], sem))

    def start(r, carry):
        a, b = copies(r)
        a.start()
        b.start()
        return carry

    def wait(r, carry):
        a, b = copies(r)
        a.wait()
        b.wait()
        return carry

    lax.fori_loop(0, tm, start, 0, unroll=4)
    lax.fori_loop(0, tm, wait, 0, unroll=4)


def _dispatch(h2, d0, d1, xs_buf, tm):
    n = h2.shape[0]
    return pl.pallas_call(
        functools.partial(_dispatch_kernel, tm=tm),
        grid=(n // tm,),
        in_specs=[
            pl.BlockSpec((tm,), lambda i: (i,), memory_space=pltpu.SMEM),
            pl.BlockSpec((tm,), lambda i: (i,), memory_space=pltpu.SMEM),
            pl.BlockSpec((tm, D), lambda i: (i, 0)),
            pl.BlockSpec(memory_space=pl.ANY),
        ],
        out_specs=pl.BlockSpec(memory_space=pl.ANY),
        out_shape=jax.ShapeDtypeStruct(xs_buf.shape, F32),
        input_output_aliases={3: 0},
        scratch_shapes=[pltpu.SemaphoreType.DMA],
        compiler_params=_cparams("arbitrary"),
        name="dispatch",
    )(d0, d1, h2, xs_buf)


def _expert_kernel(be_ref, nblk_ref, xs_ref, wg_ref, wu_ref, wd_ref, ys_ref):
    i = pl.program_id(0)

    @pl.when(i < nblk_ref[0])
    def _():
        wg, wu, wd = wg_ref[0, 0].astype(BF16), wu_ref[0, 0].astype(BF16), wd_ref[0, 0].astype(BF16)
        half = TM_EXP // 2
        rows = [slice(0, half), slice(half, TM_EXP)]
        xh = [xs_ref[r, :].astype(BF16) for r in rows]
        gs = [jnp.dot(x, wg, preferred_element_type=F32) for x in xh]
        us = [jnp.dot(x, wu, preferred_element_type=F32) for x in xh]
        hs = [(_silu(g) * u).astype(BF16) for g, u in zip(gs, us)]
        ys = [jnp.dot(h, wd, preferred_element_type=F32) for h in hs]
        for r, y in zip(rows, ys):
            ys_ref[r, :] = y

    @pl.when(i >= nblk_ref[0])
    def _():
        ys_ref[...] = jnp.zeros(ys_ref.shape, F32)


def _experts(xs, block_e, n_used, w_gate, w_up, w_down, l):
    n_rows = xs.shape[0]
    n_blocks = n_rows // TM_EXP
    grid_spec = pltpu.PrefetchScalarGridSpec(
        num_scalar_prefetch=2,
        grid=(n_blocks,),
        in_specs=[
            pl.BlockSpec((TM_EXP, D), lambda i, be, nb: (jnp.minimum(i, nb[0] - 1), 0)),
            pl.BlockSpec((1, 1, D, D_EXP), lambda i, be, nb: (l, be[i], 0, 0)),
            pl.BlockSpec((1, 1, D, D_EXP), lambda i, be, nb: (l, be[i], 0, 0)),
            pl.BlockSpec((1, 1, D_EXP, D), lambda i, be, nb: (l, be[i], 0, 0)),
        ],
        out_specs=pl.BlockSpec((TM_EXP, D), lambda i, be, nb: (i, 0)),
    )
    return pl.pallas_call(
        _expert_kernel,
        grid_spec=grid_spec,
        out_shape=jax.ShapeDtypeStruct((n_rows, D), F32),
        compiler_params=_cparams("arbitrary"),
        name="experts",
    )(block_e, n_used, xs, w_gate, w_up, w_down)


def _combine_kernel(d0_ref, d1_ref, n0_ref, n1_ref, ys_hbm, rw_ref, x_ref, g2_ref, lg_ref, lb_ref, o_ref,
                    y0, y1, sem, *, mode, alpha, tm, nsteps):
    i = pl.program_id(0)
    slot = i % 2

    def copies(i0_ref, i1_ref, s, r):
        return (pltpu.make_async_copy(ys_hbm.at[pl.ds(i0_ref[r], 1), :], y0.at[s, pl.ds(r, 1), :], sem.at[s]),
                pltpu.make_async_copy(ys_hbm.at[pl.ds(i1_ref[r], 1), :], y1.at[s, pl.ds(r, 1), :], sem.at[s]))

    def request(i0_ref, i1_ref, s):
        def body(r, carry):
            a, b = copies(i0_ref, i1_ref, s, r)
            a.start()
            b.start()
            return carry
        lax.fori_loop(0, tm, body, 0, unroll=4)

    @pl.when(i == 0)
    def _():
        request(d0_ref, d1_ref, slot)

    @pl.when(i + 1 < nsteps)
    def _():
        request(n0_ref, n1_ref, 1 - slot)

    def wait(r, carry):
        a, b = copies(d0_ref, d1_ref, slot, r)
        a.wait()
        b.wait()
        return carry

    lax.fori_loop(0, tm, wait, 0, unroll=4)
    rw = rw_ref[...]
    y = rw[:, 0:1] * y0[slot] + rw[:, 1:2] * y1[slot]
    o_ref[...] = _layer_norm(alpha * x_ref[...] + (1.0 + _mod_val(g2_ref, mode)) * y,
                             lg_ref[0], lb_ref[0])


def _combine(ys, d0, d1, rw, x1, mods, ln_g, ln_b, l, mode, tm, tiles_per_batch, alpha):
    n = x1.shape[0]
    nsteps = n // tm
    nxt = lambda i: (jnp.minimum(i + 1, nsteps - 1),)
    return pl.pallas_call(
        functools.partial(_combine_kernel, mode=mode, alpha=alpha, tm=tm, nsteps=nsteps),
        grid=(nsteps,),
        in_specs=[
            pl.BlockSpec((tm,), lambda i: (i,), memory_space=pltpu.SMEM),
            pl.BlockSpec((tm,), lambda i: (i,), memory_space=pltpu.SMEM),
            pl.BlockSpec((tm,), nxt, memory_space=pltpu.SMEM),
            pl.BlockSpec((tm,), nxt, memory_space=pltpu.SMEM),
            pl.BlockSpec(memory_space=pl.ANY),
            pl.BlockSpec((tm, LANES), lambda i: (i, 0)),
            pl.BlockSpec((tm, D), lambda i: (i, 0)),
            _mod_spec(mode, l, tm, 5, tiles_per_batch),
            _layer_spec((1, D), l),
            _layer_spec((1, D), l),
        ],
        out_specs=pl.BlockSpec((tm, D), lambda i: (i, 0)),
        out_shape=jax.ShapeDtypeStruct((n, D), F32),
        scratch_shapes=[pltpu.VMEM((2, tm, D), F32), pltpu.VMEM((2, tm, D), F32),
                        pltpu.SemaphoreType.DMA((2,))],
        compiler_params=_cparams("arbitrary"),
        name="combine",
    )(d0, d1, d0, d1, ys, rw, x1, mods, ln_g, ln_b)


def _route_meta(eid):
    n = eid.shape[0]
    flat = eid.reshape(-1)
    n_slots = flat.shape[0]
    onehot = (flat[:, None] == jnp.arange(N_EXP, dtype=I32)[None, :]).astype(I32)
    csum = jnp.cumsum(onehot, axis=0)
    counts = csum[-1]
    rank = jnp.take_along_axis(csum - onehot, flat[:, None], axis=1)[:, 0]
    padded = (counts + TM_EXP - 1) // TM_EXP * TM_EXP
    pad_end = jnp.cumsum(padded)
    pad_start = pad_end - padded
    dest = pad_start[flat] + rank
    n_blocks = (n_slots + N_EXP * (TM_EXP - 1)) // TM_EXP
    block_start = jnp.arange(n_blocks, dtype=I32) * TM_EXP
    block_e = jnp.minimum(jnp.sum(pad_end[None, :] <= block_start[:, None], axis=1), N_EXP - 1).astype(I32)
    n_used = (pad_end[-1:] // TM_EXP).astype(I32)
    dest = dest.reshape(n, 2)
    return n_blocks * TM_EXP, block_e, n_used, dest[:, 0], dest[:, 1]


def _pack_w_in(w_in):
    off_g_a = CONV_DIM
    off_decay = off_g_a + D
    off_q_b = off_decay + 2 * NH
    off_merge = off_q_b + 4 * D
    depth = w_in.shape[0]
    pad = jnp.zeros((depth, D, LANES - 2 * NH), w_in.dtype)
    small = jnp.concatenate([w_in[:, :, off_decay:off_q_b], pad], axis=2)
    main = jnp.concatenate([w_in[:, :, :off_decay], w_in[:, :, off_q_b:off_merge + 2 * D]], axis=2)
    return main.astype(BF16), small.astype(BF16)


def _pad_lanes(v):
    depth, k = v.shape
    return jnp.concatenate([v, jnp.zeros((depth, LANES - k), v.dtype)], axis=1).reshape(depth, 1, LANES)


def kernel(x_prompt, x_sample, state_conv, state_delta, state_hgrn, c_prompt, c_sample, ln0_g, ln0_b, w_ada, b_ada, w_in, conv_w, a_log, dt_bias, norm_a, lb_param, norm_b, w_br_a, w_br_b, w_o, ln1_g, ln1_b, w_grp, b_grp, w_rt, b_rt, w_gate_e, w_up_e, w_down_e, ln2_g, ln2_b):
    depth = w_in.shape[0]
    alpha = (2 * depth) ** 0.25
    bp, tp, _ = x_prompt.shape
    bs, ts, _ = x_sample.shape
    n_p = bp * tp
    tsp = SUBLANES
    assert ts <= tsp
    n_s = bs * tsp

    w_main, w_small = _pack_w_in(w_in)
    w_a16, w_b16, w_o16 = w_br_a.astype(BF16), w_br_b.astype(BF16), w_o.astype(BF16)
    w_r = jnp.concatenate([w_grp, w_rt, jnp.zeros((depth, D, LANES - N_GROUPS - N_EXP), F32)], axis=2)
    b_r = jnp.concatenate([b_grp, b_rt, jnp.zeros((depth, LANES - N_GROUPS - N_EXP), F32)], axis=1)
    b_r = b_r.reshape(depth, 1, LANES)
    a_log_p, dt_bias_p = _pad_lanes(a_log), _pad_lanes(dt_bias)
    norm_a3, norm_b3 = norm_a.reshape(depth, 1, HD), norm_b.reshape(depth, 1, HD)
    ln1_g3, ln1_b3 = ln1_g.reshape(depth, 1, D), ln1_b.reshape(depth, 1, D)
    ln2_g3, ln2_b3 = ln2_g.reshape(depth, 1, D), ln2_b.reshape(depth, 1, D)

    mods = _modulation(jnp.concatenate([c_prompt, c_sample], axis=0), w_ada, b_ada)
    mods_p = mods[:, :bp].reshape(depth, bp, 1, 6 * D)
    mods_s = jnp.repeat(mods[:, bp:], tsp, axis=1)

    xs_pad = jnp.concatenate([x_sample, jnp.zeros((bs, tsp - ts, D), x_sample.dtype)], axis=1)
    cp = math.gcd(tp, 128)
    tm_p = math.gcd(tp, 512)
    tm_s = math.gcd(n_s, 512)
    bb_s = math.gcd(bs, 8)
    state_shape = lambda nb: (depth, nb, NH, HD, HD)
    conv_shape = lambda nb: (depth, nb, CONV_W - 1, CONV_DIM)
    groups = [
        dict(mode="row", n=n_p, nb=bp, t=tp, C=cp, tv=cp, tm=tm_p, tpb=tp // tm_p, bb=1, mods=mods_p,
             conv=jnp.zeros(conv_shape(bp), F32), delta=jnp.zeros(state_shape(bp), F32),
             hgrn=jnp.zeros(state_shape(bp), F32)),
        dict(mode="tok", n=n_s, nb=bs, t=tsp, C=tsp, tv=ts, tm=tm_s, tpb=1, bb=bb_s, mods=mods_s,
             conv=state_conv, delta=state_delta, hgrn=state_hgrn),
    ]
    for g in groups:
        g["tm_dma"] = math.gcd(g["t"] if g["mode"] == "row" else g["n"], 1024)
    for g in groups:
        g["conv_o"] = jnp.zeros(conv_shape(g["nb"]), F32)
        g["delta_o"] = jnp.zeros(state_shape(g["nb"]), F32)
        g["hgrn_o"] = jnp.zeros(state_shape(g["nb"]), F32)
    xs = [_ln0(x_prompt.reshape(n_p, D), ln0_g, ln0_b, tm_p),
          _ln0(xs_pad.reshape(n_s, D), ln0_g, ln0_b, tm_s)]

    for l in range(depth):
        x1s, h2s, rws, res = [], [], [], []
        for gi, g in enumerate(groups):
            tm_in = math.gcd(g["t"] if g["mode"] == "row" else g["n"], 1024)
            if g["mode"] == "row":
                proj, small = _inproj(xs[gi], g["mods"], w_main, w_small, l, g["mode"], tm_in,
                                      g["t"] // tm_in)
                proj3 = proj.reshape(g["nb"], g["t"], NPROJ)
                small3 = small.reshape(g["nb"], g["t"], LANES)
            else:
                proj3, small3 = _inproj(xs[gi], g["mods"], w_main, w_small, l, g["mode"], tm_in, 1,
                                        t3=g["t"])
                proj = proj3
            o_a, g["conv_o"], g["delta_o"] = _mixer_a(
                proj3, small3, g["conv"], g["delta"], g["conv_o"], g["delta_o"], conv_w, a_log_p, dt_bias_p,
                norm_a3, l, g["C"], g["tv"], g["bb"])
            o_b, g["hgrn_o"] = _mixer_b(proj3, g["hgrn"], g["hgrn_o"], lb_param, norm_b3, l, g["C"],
                                        g["tv"], g["bb"])
            halve = 1 if g["mode"] == "row" else 2
            if g["mode"] == "row":
                o_a, o_b = o_a.reshape(g["n"], D), o_b.reshape(g["n"], D)
            x1, h2, rw, re = _post_mixer(o_a, o_b, proj, xs[gi],
                                         g["mods"], w_a16, w_b16, w_o16, ln1_g3, ln1_b3, w_r, b_r, l,
                                         g["mode"], g["tm"] // halve, g["tpb"] * halve, alpha)
            x1s.append(x1)
            h2s.append(h2)
            rws.append(rw)
            res.append(re)
        eid = jnp.concatenate([r[:, :2] for r in res], axis=0)
        n_rows, block_e, n_used, d0, d1 = _route_meta(eid)
        if l == 0:
            xs_sorted = jnp.zeros((n_rows, D), F32)
        off = 0
        for gi, g in enumerate(groups):
            sl = slice(off, off + g["n"])
            xs_sorted = _dispatch(h2s[gi], d0[sl], d1[sl], xs_sorted, g["tm_dma"])
            off += g["n"]
        ys = _experts(xs_sorted, block_e, n_used, w_gate_e, w_up_e, w_down_e, l)
        off = 0
        for gi, g in enumerate(groups):
            sl = slice(off, off + g["n"])
            xs[gi] = _combine(ys, d0[sl], d1[sl], rws[gi], x1s[gi], g["mods"], ln2_g3, ln2_b3, l,
                              g["mode"], g["tm_dma"], g["t"] // g["tm_dma"], alpha)
            off += g["n"]

    y_prompt = xs[0].reshape(bp, tp, D)
    y_sample = xs[1].reshape(bs, tsp, D)[:, :ts]
    gp, gs = groups
    return (y_prompt, y_sample, gp["conv_o"], gp["delta_o"], gp["hgrn_o"],
            gs["conv_o"], gs["delta_o"], gs["hgrn_o"])
```

```python
import functools
import math

import jax
import jax.numpy as jnp
from jax import lax
from jax.experimental import pallas as pl
from jax.experimental.pallas import tpu as pltpu

F32 = jnp.float32
BF16 = jnp.bfloat16
I32 = jnp.int32

D = 1024
NH = 8
HD = 128
CONV_W = 4
CONV_DIM = 3 * D
N_GROUPS = 4
EPG = 8
N_EXP = N_GROUPS * EPG
D_EXP = 512
F_MIN = 1e-30
LN_EPS = 1e-5
RMS_EPS = 1e-6

C_QKV = 0
C_GA = 3 * D
C_QB = 4 * D
C_FB = 5 * D
C_IB = 6 * D
C_GB = 7 * D
C_MERGE = 8 * D
NPROJ = 10 * D

LANES = 128
SUBLANES = 8
VMEM_LIMIT = 48 * 1024 * 1024

HI = lax.Precision.HIGHEST


def _cparams(*sem):
    return pltpu.CompilerParams(dimension_semantics=sem, vmem_limit_bytes=VMEM_LIMIT)


def _dot(a, b, hi=False):
    if hi:
        return jnp.dot(a, b, preferred_element_type=F32, precision=HI)
    return jnp.dot(a.astype(BF16), b.astype(BF16), preferred_element_type=F32)


def _dot_nt(a, b):
    return lax.dot_general(a.astype(BF16), b.astype(BF16), (((1,), (1,)), ((), ())),
                           preferred_element_type=F32)


def _dot_tn(a, b):
    return lax.dot_general(a.astype(BF16), b.astype(BF16), (((0,), (0,)), ((), ())),
                           preferred_element_type=F32)


def _split(a):
    hi = a.astype(BF16)
    return hi, (a - hi.astype(F32)).astype(BF16)


def _dot3(a, b):
    mm = lambda x, y: jnp.dot(x, y, preferred_element_type=F32)
    return mm(a[0], b[0]) + (mm(a[0], b[1]) + mm(a[1], b[0]))


def _sigmoid(x):
    return 1.0 / (1.0 + jnp.exp(-x))


def _silu(x):
    return x * _sigmoid(x)


def _softplus(x):
    return jnp.maximum(x, 0.0) + jnp.log(1.0 + jnp.exp(-jnp.abs(x)))


def _layer_norm(x, g, b):
    mu = jnp.mean(x, -1, keepdims=True)
    xc = x - mu
    var = jnp.mean(xc * xc, -1, keepdims=True)
    return xc * lax.rsqrt(var + LN_EPS) * g + b


def _layer_spec(shape, l):
    zeros = (0,) * len(shape)
    return pl.BlockSpec((1,) + tuple(shape), lambda *_: (l,) + zeros)


def _mod_kernel(c_ref, w_ref, b_ref, o_ref):
    ca = _silu(c_ref[...])
    o_ref[0] = _dot(ca, w_ref[0]) + b_ref[0]


def _modulation(c_all, w_ada, b_ada):
    depth = w_ada.shape[0]
    nb = c_all.shape[0]
    tn = 1536
    return pl.pallas_call(
        _mod_kernel,
        grid=(depth, 6 * D // tn),
        in_specs=[
            pl.BlockSpec((nb, D), lambda l, j: (0, 0)),
            pl.BlockSpec((1, D, tn), lambda l, j: (l, 0, j)),
            pl.BlockSpec((1, 1, tn), lambda l, j: (l, 0, j)),
        ],
        out_specs=pl.BlockSpec((1, nb, tn), lambda l, j: (l, 0, j)),
        out_shape=jax.ShapeDtypeStruct((depth, nb, 6 * D), F32),
        compiler_params=_cparams("parallel", "parallel"),
        name="adaln_mod",
    )(c_all, w_ada, b_ada.reshape(depth, 1, 6 * D))


def _mod_spec(mode, l, tm, k, tiles_per_batch):
    if mode == "row":
        return pl.BlockSpec((1, 1, 1, D), lambda i, *_: (l, i // tiles_per_batch, 0, k))
    return pl.BlockSpec((1, tm, D), lambda i, *_: (l, i, k))


def _mod_val(ref, mode):
    return ref[0, 0] if mode == "row" else ref[0]


def _ln0_kernel(x_ref, g_ref, b_ref, o_ref):
    o_ref[...] = _layer_norm(x_ref[...], g_ref[...], b_ref[...])


def _ln0(x, g, b, tm):
    n = x.shape[0]
    return pl.pallas_call(
        _ln0_kernel,
        grid=(n // tm,),
        in_specs=[pl.BlockSpec((tm, D), lambda i: (i, 0)),
                  pl.BlockSpec((1, D), lambda i: (0, 0)),
                  pl.BlockSpec((1, D), lambda i: (0, 0))],
        out_specs=pl.BlockSpec((tm, D), lambda i: (i, 0)),
        out_shape=jax.ShapeDtypeStruct((n, D), F32),
        compiler_params=_cparams("parallel"),
        name="ln0",
    )(x, g.reshape(1, D), b.reshape(1, D))


def _inproj_kernel(x_ref, sc_ref, sh_ref, w_ref, ws_ref, o_ref, os_ref, h_scr, *, mode):
    @pl.when(pl.program_id(1) == 0)
    def _():
        h = x_ref[...] * (1.0 + _mod_val(sc_ref, mode)) + _mod_val(sh_ref, mode)
        h_scr[...] = h.astype(BF16)
        os_ref[...] = jnp.dot(h_scr[...], ws_ref[0], preferred_element_type=F32).reshape(os_ref.shape)

    o_ref[...] = jnp.dot(h_scr[...], w_ref[0], preferred_element_type=F32).reshape(o_ref.shape)


def _inproj(x, mods, w_main, w_small, l, mode, tm, tiles_per_batch, t3=None):
    n = x.shape[0]
    tn = 1280
    if t3 is None:
        out_specs = [pl.BlockSpec((tm, tn), lambda i, j: (i, j)),
                     pl.BlockSpec((tm, LANES), lambda i, j: (i, 0))]
        out_shape = [jax.ShapeDtypeStruct((n, NPROJ), F32), jax.ShapeDtypeStruct((n, LANES), F32)]
    else:
        out_specs = [pl.BlockSpec((tm // t3, t3, tn), lambda i, j: (i, 0, j)),
                     pl.BlockSpec((tm // t3, t3, LANES), lambda i, j: (i, 0, 0))]
        out_shape = [jax.ShapeDtypeStruct((n // t3, t3, NPROJ), F32),
                     jax.ShapeDtypeStruct((n // t3, t3, LANES), F32)]
    return pl.pallas_call(
        functools.partial(_inproj_kernel, mode=mode),
        grid=(n // tm, NPROJ // tn),
        in_specs=[
            pl.BlockSpec((tm, D), lambda i, j: (i, 0)),
            _mod_spec(mode, l, tm, 1, tiles_per_batch),
            _mod_spec(mode, l, tm, 0, tiles_per_batch),
            pl.BlockSpec((1, D, tn), lambda i, j: (l, 0, j)),
            pl.BlockSpec((1, D, LANES), lambda i, j: (l, 0, 0)),
        ],
        out_specs=out_specs,
        out_shape=out_shape,
        scratch_shapes=[pltpu.VMEM((tm, D), BF16)],
        compiler_params=_cparams("parallel", "arbitrary"),
        name="inproj",
    )(x, mods, mods, w_main, w_small)


def _mixa_kernel(qkv_ref, ga_ref, sm_ref, cst_ref, s0_ref, cw_ref, alog_ref, dtb_ref, na_ref,
                 cbuf_ref, sbuf_ref, oa_ref, cso_ref, so_ref, xbuf, s_scr, *, C, tv, nc, bb):
    del cbuf_ref, sbuf_ref
    c = pl.program_id(1)
    row = lax.broadcasted_iota(I32, (C, 1), 0)
    ri = lax.broadcasted_iota(I32, (C, C), 0)
    ci = lax.broadcasted_iota(I32, (C, C), 1)
    causal = ri >= ci
    strict = ri > ci
    eye = (ri == ci).astype(F32)
    tril = causal.astype(F32)
    hist = SUBLANES - (CONV_W - 1)

    @pl.when(c == 0)
    def _():
        for bi in range(bb):
            xbuf[bi, 0:SUBLANES, :] = jnp.zeros((SUBLANES, CONV_DIM), F32)
            xbuf[bi, hist:SUBLANES, :] = cst_ref[0, bi]
            s_scr[bi] = s0_ref[0, bi]

    pre = []
    for bi in range(bb):
        xbuf[bi, SUBLANES:SUBLANES + C, :] = qkv_ref[bi]
        sm = sm_ref[bi]
        g_all = -jnp.exp(alog_ref[0]) * _softplus(sm + dtb_ref[0])
        beta_all = _sigmoid(sm)
        if tv < C:
            g_all = jnp.where(row < tv, g_all, 0.0)
            beta_all = jnp.where(row < tv, beta_all, 0.0)
        gcum_all = _dot(tril, g_all, hi=True)
        pre.append((gcum_all, gcum_all.T, jnp.exp(gcum_all), beta_all))

    n_nil = min(C, tv)
    n_double = max(int(math.ceil(math.log2(n_nil))) - 1, 0)
    units = [(bi, h) for bi in range(bb) for h in range(NH)]

    st = []
    for bi, h in units:
        gcum_all, gcum_t, eg_all, beta_all = pre[bi]
        conv = []
        for part in range(3):
            col = slice(part * D + h * HD, part * D + (h + 1) * HD)
            acc = None
            for j in range(CONV_W):
                term = xbuf[bi, hist + j:hist + j + C, col] * cw_ref[0, j:j + 1, col]
                acc = term if acc is None else acc + term
            conv.append(_silu(acc))
        q, k, v = conv
        q = q * lax.rsqrt(jnp.sum(q * q, -1, keepdims=True) + RMS_EPS) * (HD ** -0.5)
        k = k * lax.rsqrt(jnp.sum(k * k, -1, keepdims=True) + RMS_EPS)
        gc = gcum_all[:, h:h + 1]
        gr = gcum_t[h:h + 1, :]
        bt = beta_all[:, NH + h:NH + h + 1]
        eg = eg_all[:, h:h + 1]
        glast = gcum_all[tv - 1:tv, h:h + 1]
        dec = jnp.where(causal, jnp.exp(jnp.where(causal, gc - gr, 0.0)), 0.0)
        kb = k * bt
        kk = _dot_nt(jnp.concatenate([kb, q], axis=0), k)
        st.append(dict(
            a=-jnp.where(strict, kk[:C] * dec, 0.0),
            rhs=jnp.concatenate([kb * eg, v * bt], axis=1),
            qk=jnp.where(causal, kk[C:] * dec, 0.0),
            qe=q * eg,
            kg=k * jnp.exp(glast - gc),
            eglast=jnp.exp(glast),
        ))

    ps = [eye + u["a"] for u in st]
    qsp = [_split(u["a"]) for u in st]
    for _ in range(n_double):
        qs = [_dot3(s2, s2) for s2 in qsp]
        qsp = [_split(x) for x in qs]
        ps = [p + _dot3(_split(p), s2) for p, s2 in zip(ps, qsp)]
    sols = [_dot3(_split(p), _split(u["rhs"])) for p, u in zip(ps, st)]

    ss = [s_scr[bi, h] for bi, h in units]
    gates = [_silu(ga_ref[bi, :, h * HD:(h + 1) * HD]) for bi, h in units]
    wss = [_dot(jnp.concatenate([sol[:, :HD], u["qe"]], axis=0), s)
           for u, sol, s in zip(st, sols, ss)]
    vns = [sol[:, HD:] - ws[:C] for sol, ws in zip(sols, wss)]
    os_ = [ws[C:] + _dot(u["qk"], vn) for u, ws, vn in zip(st, wss, vns)]
    s_new = [u["eglast"] * s + _dot_tn(u["kg"], vn) for u, s, vn in zip(st, ss, vns)]
    ys = [o * lax.rsqrt(jnp.mean(o * o, -1, keepdims=True) + RMS_EPS) * na_ref[0] * gt
          for o, gt in zip(os_, gates)]
    for (bi, h), sn, y in zip(units, s_new, ys):
        s_scr[bi, h] = sn
        oa_ref[bi, :, h * HD:(h + 1) * HD] = y

    if nc > 1:
        for bi in range(bb):
            xbuf[bi, 0:SUBLANES, :] = xbuf[bi, C:C + SUBLANES, :]

    @pl.when(c == nc - 1)
    def _():
        for bi in range(bb):
            if nc > 1:
                cso_ref[0, bi] = xbuf[bi, hist:SUBLANES, :]
            else:
                cso_ref[0, bi] = xbuf[bi, SUBLANES + tv - (CONV_W - 1):SUBLANES + tv, :]
            so_ref[0, bi] = s_scr[bi]


def _mixer_a(proj3, small3, conv_state, s0, conv_buf, s_buf, conv_w, a_log_p, dt_bias_p, norm_a, l, C, tv,
             bb):
    nb, t, _ = proj3.shape
    nc = t // C
    assert nc == 1 or tv == C
    return pl.pallas_call(
        functools.partial(_mixa_kernel, C=C, tv=tv, nc=nc, bb=bb),
        grid=(nb // bb, nc),
        in_specs=[
            pl.BlockSpec((bb, C, CONV_DIM), lambda b, c: (b, c, C_QKV // CONV_DIM)),
            pl.BlockSpec((bb, C, D), lambda b, c: (b, c, C_GA // D)),
            pl.BlockSpec((bb, C, LANES), lambda b, c: (b, c, 0)),
            pl.BlockSpec((1, bb, CONV_W - 1, CONV_DIM), lambda b, c: (l, b, 0, 0)),
            pl.BlockSpec((1, bb, NH, HD, HD), lambda b, c: (l, b, 0, 0, 0)),
            _layer_spec((CONV_W, CONV_DIM), l),
            _layer_spec((1, LANES), l),
            _layer_spec((1, LANES), l),
            _layer_spec((1, HD), l),
            pl.BlockSpec(memory_space=pl.ANY),
            pl.BlockSpec(memory_space=pl.ANY),
        ],
        out_specs=[
            pl.BlockSpec((bb, C, D), lambda b, c: (b, c, 0)),
            pl.BlockSpec((1, bb, CONV_W - 1, CONV_DIM), lambda b, c: (l, b, 0, 0)),
            pl.BlockSpec((1, bb, NH, HD, HD), lambda b, c: (l, b, 0, 0, 0)),
        ],
        out_shape=[
            jax.ShapeDtypeStruct((nb, t, D), F32),
            jax.ShapeDtypeStruct(conv_buf.shape, F32),
            jax.ShapeDtypeStruct(s_buf.shape, F32),
        ],
        input_output_aliases={9: 1, 10: 2},
        scratch_shapes=[pltpu.VMEM((bb, C + 2 * SUBLANES, CONV_DIM), F32),
                        pltpu.VMEM((bb, NH, HD, HD), F32)],
        compiler_params=_cparams("parallel", "arbitrary"),
        name="mixer_a",
    )(proj3, proj3, small3, conv_state, s0, conv_w, a_log_p, dt_bias_p, norm_a, conv_buf, s_buf)


DIAG = SUBLANES


def _mixb_kernel(qb_ref, fb_ref, ib_ref, gb_ref, s0_ref, lbp_ref, nb_ref, sbuf_ref,
                 ob_ref, so_ref, st_scr, *, C, tv, nc, bb, layer):
    del sbuf_ref
    c = pl.program_id(1)
    row = lax.broadcasted_iota(I32, (C, 1), 0)
    ri = lax.broadcasted_iota(I32, (C, C), 0)
    ci = lax.broadcasted_iota(I32, (C, C), 1)
    tril = (ri >= ci).astype(F32)

    @pl.when(c == 0)
    def _():
        for bi in range(bb):
            for h in range(NH):
                st_scr[bi, h] = s0_ref[0, bi, h].T

    lbp = lbp_ref[...]
    e = jnp.exp(lbp - jnp.max(lbp, axis=0, keepdims=True))
    psm = e / jnp.sum(e, axis=0, keepdims=True)
    lb = jnp.zeros((1, D), F32)
    for j in range(1, layer + 1):
        lb = lb + psm[j:j + 1, :]

    row8 = lax.broadcasted_iota(I32, (DIAG, 1), 0)
    levels = []
    lv = DIAG
    while lv < C:
        levels.append(lv)
        lv *= 2

    units = []
    for bi in range(bb):
        z = fb_ref[bi]
        f = lb + (1.0 - lb) * _sigmoid(z)
        lf_all = jnp.log(jnp.maximum(f, F_MIN))
        kk_all = (1.0 - lb) * _sigmoid(-z)
        if tv < C:
            lf_all = jnp.where(row < tv, lf_all, 0.0)
            kk_all = jnp.where(row < tv, kk_all, 0.0)
        b_all = _dot(tril, lf_all, hi=True)

        for h in range(NH):
            cs = slice(h * HD, (h + 1) * HD)
            q = _silu(qb_ref[bi, :, cs])
            kk = kk_all[:, cs]
            iv = ib_ref[bi, :, cs]
            b = b_all[:, cs]
            blast = b[tv - 1:tv, :]

            amat = None
            for lv in levels:
                pieces = []
                for j in range(C // lv):
                    blk = b[j * lv:(j + 1) * lv, :]
                    if j % 2 == 1:
                        pieces.append(blk - b[j * lv:j * lv + 1, :])
                    else:
                        pieces.append(b[(j + 1) * lv:(j + 1) * lv + 1, :] - blk)
                ex = jnp.exp(jnp.concatenate(pieces, axis=0))
                lower = ((row // lv) % 2) == 1
                qs = jnp.where(lower, q * ex, 0.0)
                ks = jnp.where(lower, 0.0, kk * ex)
                blk_a = jnp.where((ri // (2 * lv)) == (ci // (2 * lv)), _dot_nt(qs, ks), 0.0)
                amat = blk_a if amat is None else amat + blk_a

            od = []
            for j in range(C // DIAG):
                rs = slice(j * DIAG, (j + 1) * DIAG)
                bd, qq, k8, i8 = b[rs, :], q[rs, :], kk[rs, :], iv[rs, :]
                acc = jnp.zeros((DIAG, HD), F32)
                for s in range(DIAG):
                    ex = jnp.exp(jnp.where(row8 >= s, bd - bd[s:s + 1, :], -jnp.inf))
                    zz = qq * ex * k8[s:s + 1, :]
                    acc = acc + jnp.sum(zz, -1, keepdims=True) * i8[s:s + 1, :]
                od.append(acc)
            o = od[0] if len(od) == 1 else jnp.concatenate(od, axis=0)
            if amat is not None:
                o = o + _dot(amat, iv)

            units.append(dict(bi=bi, h=h, o=o, qe=q * jnp.exp(b), kg=kk * jnp.exp(blast - b), iv=iv,
                              eb=jnp.exp(blast), gate=_silu(gb_ref[bi, :, cs])))

    sts = [st_scr[u["bi"], u["h"]] for u in units]
    os_ = [u["o"] + _dot_nt(u["qe"], st) for u, st in zip(units, sts)]
    st_new = [st * u["eb"] + _dot_tn(u["iv"], u["kg"]) for u, st in zip(units, sts)]
    ys = [o * lax.rsqrt(jnp.mean(o * o, -1, keepdims=True) + RMS_EPS) * nb_ref[0] * u["gate"]
          for u, o in zip(units, os_)]
    for u, sn, y in zip(units, st_new, ys):
        st_scr[u["bi"], u["h"]] = sn
        ob_ref[u["bi"], :, u["h"] * HD:(u["h"] + 1) * HD] = y

    @pl.when(c == nc - 1)
    def _():
        for bi in range(bb):
            for h in range(NH):
                so_ref[0, bi, h] = st_scr[bi, h].T


def _mixer_b(proj3, s0, s_buf, lb_param, norm_b, l, C, tv, bb):
    nb, t, _ = proj3.shape
    nc = t // C
    depth = lb_param.shape[0]
    tok = lambda col: pl.BlockSpec((bb, C, D), lambda b, c: (b, c, col // D))
    return pl.pallas_call(
        functools.partial(_mixb_kernel, C=C, tv=tv, nc=nc, bb=bb, layer=l),
        grid=(nb // bb, nc),
        in_specs=[
            tok(C_QB), tok(C_FB), tok(C_IB), tok(C_GB),
            pl.BlockSpec((1, bb, NH, HD, HD), lambda b, c: (l, b, 0, 0, 0)),
            pl.BlockSpec((depth, D), lambda b, c: (0, 0)),
            _layer_spec((1, HD), l),
            pl.BlockSpec(memory_space=pl.ANY),
        ],
        out_specs=[
            pl.BlockSpec((bb, C, D), lambda b, c: (b, c, 0)),
            pl.BlockSpec((1, bb, NH, HD, HD), lambda b, c: (l, b, 0, 0, 0)),
        ],
        out_shape=[
            jax.ShapeDtypeStruct((nb, t, D), F32),
            jax.ShapeDtypeStruct(s_buf.shape, F32),
        ],
        input_output_aliases={7: 1},
        scratch_shapes=[pltpu.VMEM((bb, NH, HD, HD), F32)],
        compiler_params=_cparams("parallel", "arbitrary"),
        name="mixer_b",
    )(proj3, proj3, proj3, proj3, s0, lb_param, norm_b, s_buf)


def _post_kernel(oa_ref, ob_ref, gt_ref, x_ref, g1_ref, sc2_ref, sh2_ref, wa_ref, wb_ref, wo_ref,
                 lg_ref, lb_ref, wr_ref, br_ref, x1_ref, h2_ref, rw_ref, re_ref, *, mode, alpha, nsub):
    sub = x_ref.shape[0] // nsub
    rows = [slice(k * sub, (k + 1) * sub) for k in range(nsub)]
    mod = lambda ref, r: ref[0, 0] if mode == "row" else ref[0, r, :]

    def tok(ref, k):
        if len(ref.shape) == 2:
            return ref[rows[k], :]
        t3 = ref.shape[1]
        return ref[k * sub // t3:(k + 1) * sub // t3].reshape(sub, ref.shape[2])

    gates = [_sigmoid(tok(gt_ref, k)) for k in range(nsub)]
    pa = [_dot(tok(oa_ref, k), wa_ref[0]) for k in range(nsub)]
    pb = [_dot(tok(ob_ref, k), wb_ref[0]) for k in range(nsub)]
    ys = [_dot(g[:, :D] * a + g[:, D:] * b, wo_ref[0]) for g, a, b in zip(gates, pa, pb)]
    x1s = [_layer_norm(alpha * x_ref[r, :] + (1.0 + mod(g1_ref, r)) * y, lg_ref[0], lb_ref[0])
           for r, y in zip(rows, ys)]
    h2s = [x1 * (1.0 + mod(sc2_ref, r)) + mod(sh2_ref, r) for r, x1 in zip(rows, x1s)]
    lgs = [_dot(h2, wr_ref[0], hi=True) + br_ref[0] for h2 in h2s]
    for r, x1, h2, logits in zip(rows, x1s, h2s, lgs):
        x1_ref[r, :] = x1
        h2_ref[r, :] = h2
        rw, re = _route(logits)
        rw_ref[r, :] = rw
        re_ref[r, :] = re


def _route(logits):
    lane = lax.broadcasted_iota(I32, logits.shape, 1)
    neg = jnp.float32(-jnp.inf)
    gl = jnp.where(lane < N_GROUPS, logits, neg)
    gmax = jnp.max(gl, -1, keepdims=True)
    grp = jnp.min(jnp.where(gl == gmax, lane, LANES), -1, keepdims=True)
    p_grp = 1.0 / jnp.sum(jnp.exp(gl - gmax), -1, keepdims=True)
    lo = N_GROUPS + grp * EPG
    el = jnp.where((lane >= lo) & (lane < lo + EPG), logits, neg)
    v1 = jnp.max(el, -1, keepdims=True)
    i1 = jnp.min(jnp.where(el == v1, lane, LANES), -1, keepdims=True)
    el2 = jnp.where(lane == i1, neg, el)
    v2 = jnp.max(el2, -1, keepdims=True)
    i2 = jnp.min(jnp.where(el2 == v2, lane, LANES), -1, keepdims=True)
    e2 = jnp.exp(v2 - v1)
    w1 = p_grp / (1.0 + e2)
    w2 = p_grp * e2 / (1.0 + e2)
    return (jnp.where(lane == 0, w1, jnp.where(lane == 1, w2, 0.0)),
            jnp.where(lane == 0, i1 - N_GROUPS, jnp.where(lane == 1, i2 - N_GROUPS, 0)))


def _post_mixer(o_a, o_b, proj, x, mods, w_br_a, w_br_b, w_o, ln_g, ln_b, w_r, b_r, l, mode, tm,
                tiles_per_batch, alpha):
    n = x.shape[0]
    tok = pl.BlockSpec((tm, D), lambda i: (i, 0))
    lane_out = pl.BlockSpec((tm, LANES), lambda i: (i, 0))
    if proj.ndim == 2:
        mix = tok
        gate_spec = pl.BlockSpec((tm, 2 * D), lambda i: (i, C_MERGE // (2 * D)))
    else:
        t3 = proj.shape[1]
        mix = pl.BlockSpec((tm // t3, t3, D), lambda i: (i, 0, 0))
        gate_spec = pl.BlockSpec((tm // t3, t3, 2 * D), lambda i: (i, 0, C_MERGE // (2 * D)))
    return pl.pallas_call(
        functools.partial(_post_kernel, mode=mode, alpha=alpha, nsub=4),
        grid=(n // tm,),
        in_specs=[
            mix, mix,
            gate_spec,
            tok,
            _mod_spec(mode, l, tm, 2, tiles_per_batch),
            _mod_spec(mode, l, tm, 4, tiles_per_batch),
            _mod_spec(mode, l, tm, 3, tiles_per_batch),
            _layer_spec((D, D), l), _layer_spec((D, D), l), _layer_spec((D, D), l),
            _layer_spec((1, D), l), _layer_spec((1, D), l),
            _layer_spec((D, LANES), l), _layer_spec((1, LANES), l),
        ],
        out_specs=[tok, tok, lane_out, lane_out],
        out_shape=[
            jax.ShapeDtypeStruct((n, D), F32),
            jax.ShapeDtypeStruct((n, D), F32),
            jax.ShapeDtypeStruct((n, LANES), F32),
            jax.ShapeDtypeStruct((n, LANES), I32),
        ],
        compiler_params=_cparams("parallel"),
        name="post_mixer",
    )(o_a, o_b, proj, x, mods, mods, mods, w_br_a, w_br_b, w_o, ln_g, ln_b, w_r, b_r)


TM_EXP = 256


def _dispatch_kernel(d0_ref, d1_ref, h_ref, xs_in, xs_out, sem, *, tm):
    del xs_in

    def copies(r):
        src = h_ref.at[pl.ds(r, 1), :]
        return (pltpu.make_async_copy(src, xs_out.at[pl.ds(d0_ref[r], 1), :], sem),
                pltpu.make_async_copy(src, xs_out.at[pl.ds(d1_ref[r], 1), :], sem))

    def start(r, carry):
        a, b = copies(r)
        a.start()
        b.start()
        return carry

    def wait(r, carry):
        a, b = copies(r)
        a.wait()
        b.wait()
        return carry

    lax.fori_loop(0, tm, start, 0, unroll=4)
    lax.fori_loop(0, tm, wait, 0, unroll=4)


def _dispatch(h2, d0, d1, xs_buf, tm):
    n = h2.shape[0]
    return pl.pallas_call(
        functools.partial(_dispatch_kernel, tm=tm),
        grid=(n // tm,),
        in_specs=[
            pl.BlockSpec((tm,), lambda i: (i,), memory_space=pltpu.SMEM),
            pl.BlockSpec((tm,), lambda i: (i,), memory_space=pltpu.SMEM),
            pl.BlockSpec((tm, D), lambda i: (i, 0)),
            pl.BlockSpec(memory_space=pl.ANY),
        ],
        out_specs=pl.BlockSpec(memory_space=pl.ANY),
        out_shape=jax.ShapeDtypeStruct(xs_buf.shape, F32),
        input_output_aliases={3: 0},
        scratch_shapes=[pltpu.SemaphoreType.DMA],
        compiler_params=_cparams("arbitrary"),
        name="dispatch",
    )(d0, d1, h2, xs_buf)


def _expert_kernel(be_ref, nblk_ref, xs_ref, wg_ref, wu_ref, wd_ref, ys_ref):
    i = pl.program_id(0)

    @pl.when(i < nblk_ref[0])
    def _():
        wg, wu, wd = wg_ref[0, 0].astype(BF16), wu_ref[0, 0].astype(BF16), wd_ref[0, 0].astype(BF16)
        half = TM_EXP // 2
        rows = [slice(0, half), slice(half, TM_EXP)]
        xh = [xs_ref[r, :].astype(BF16) for r in rows]
        gs = [jnp.dot(x, wg, preferred_element_type=F32) for x in xh]
        us = [jnp.dot(x, wu, preferred_element_type=F32) for x in xh]
        hs = [(_silu(g) * u).astype(BF16) for g, u in zip(gs, us)]
        ys = [jnp.dot(h, wd, preferred_element_type=F32) for h in hs]
        for r, y in zip(rows, ys):
            ys_ref[r, :] = y

    @pl.when(i >= nblk_ref[0])
    def _():
        ys_ref[...] = jnp.zeros(ys_ref.shape, F32)


def _experts(xs, block_e, n_used, w_gate, w_up, w_down, l):
    n_rows = xs.shape[0]
    n_blocks = n_rows // TM_EXP
    grid_spec = pltpu.PrefetchScalarGridSpec(
        num_scalar_prefetch=2,
        grid=(n_blocks,),
        in_specs=[
            pl.BlockSpec((TM_EXP, D), lambda i, be, nb: (jnp.minimum(i, nb[0] - 1), 0)),
            pl.BlockSpec((1, 1, D, D_EXP), lambda i, be, nb: (l, be[i], 0, 0)),
            pl.BlockSpec((1, 1, D, D_EXP), lambda i, be, nb: (l, be[i], 0, 0)),
            pl.BlockSpec((1, 1, D_EXP, D), lambda i, be, nb: (l, be[i], 0, 0)),
        ],
        out_specs=pl.BlockSpec((TM_EXP, D), lambda i, be, nb: (i, 0)),
    )
    return pl.pallas_call(
        _expert_kernel,
        grid_spec=grid_spec,
        out_shape=jax.ShapeDtypeStruct((n_rows, D), F32),
        compiler_params=_cparams("arbitrary"),
        name="experts",
    )(block_e, n_used, xs, w_gate, w_up, w_down)


def _combine_kernel(d0_ref, d1_ref, ys_hbm, rw_ref, x_ref, g2_ref, lg_ref, lb_ref, o_ref,
                    y0, y1, sem, *, mode, alpha, tm):
    def copies(r):
        return (pltpu.make_async_copy(ys_hbm.at[pl.ds(d0_ref[r], 1), :], y0.at[pl.ds(r, 1), :], sem),
                pltpu.make_async_copy(ys_hbm.at[pl.ds(d1_ref[r], 1), :], y1.at[pl.ds(r, 1), :], sem))

    def start(r, carry):
        a, b = copies(r)
        a.start()
        b.start()
        return carry

    def wait(r, carry):
        a, b = copies(r)
        a.wait()
        b.wait()
        return carry

    lax.fori_loop(0, tm, start, 0, unroll=4)
    lax.fori_loop(0, tm, wait, 0, unroll=4)
    rw = rw_ref[...]
    y = rw[:, 0:1] * y0[...] + rw[:, 1:2] * y1[...]
    o_ref[...] = _layer_norm(alpha * x_ref[...] + (1.0 + _mod_val(g2_ref, mode)) * y,
                             lg_ref[0], lb_ref[0])


def _combine(ys, d0, d1, rw, x1, mods, ln_g, ln_b, l, mode, tm, tiles_per_batch, alpha):
    n = x1.shape[0]
    return pl.pallas_call(
        functools.partial(_combine_kernel, mode=mode, alpha=alpha, tm=tm),
        grid=(n // tm,),
        in_specs=[
            pl.BlockSpec((tm,), lambda i: (i,), memory_space=pltpu.SMEM),
            pl.BlockSpec((tm,), lambda i: (i,), memory_space=pltpu.SMEM),
            pl.BlockSpec(memory_space=pl.ANY),
            pl.BlockSpec((tm, LANES), lambda i: (i, 0)),
            pl.BlockSpec((tm, D), lambda i: (i, 0)),
            _mod_spec(mode, l, tm, 5, tiles_per_batch),
            _layer_spec((1, D), l),
            _layer_spec((1, D), l),
        ],
        out_specs=pl.BlockSpec((tm, D), lambda i: (i, 0)),
        out_shape=jax.ShapeDtypeStruct((n, D), F32),
        scratch_shapes=[pltpu.VMEM((tm, D), F32), pltpu.VMEM((tm, D), F32), pltpu.SemaphoreType.DMA],
        compiler_params=_cparams("arbitrary"),
        name="combine",
    )(d0, d1, ys, rw, x1, mods, ln_g, ln_b)


def _route_meta(eid):
    n = eid.shape[0]
    flat = eid.reshape(-1)
    n_slots = flat.shape[0]
    onehot = (flat[:, None] == jnp.arange(N_EXP, dtype=I32)[None, :]).astype(I32)
    csum = jnp.cumsum(onehot, axis=0)
    counts = csum[-1]
    rank = jnp.take_along_axis(csum - onehot, flat[:, None], axis=1)[:, 0]
    padded = (counts + TM_EXP - 1) // TM_EXP * TM_EXP
    pad_end = jnp.cumsum(padded)
    pad_start = pad_end - padded
    dest = pad_start[flat] + rank
    n_blocks = (n_slots + N_EXP * (TM_EXP - 1)) // TM_EXP
    block_start = jnp.arange(n_blocks, dtype=I32) * TM_EXP
    block_e = jnp.minimum(jnp.sum(pad_end[None, :] <= block_start[:, None], axis=1), N_EXP - 1).astype(I32)
    n_used = (pad_end[-1:] // TM_EXP).astype(I32)
    dest = dest.reshape(n, 2)
    return n_blocks * TM_EXP, block_e, n_used, dest[:, 0], dest[:, 1]


def _pack_w_in(w_in):
    off_g_a = CONV_DIM
    off_decay = off_g_a + D
    off_q_b = off_decay + 2 * NH
    off_merge = off_q_b + 4 * D
    depth = w_in.shape[0]
    pad = jnp.zeros((depth, D, LANES - 2 * NH), w_in.dtype)
    small = jnp.concatenate([w_in[:, :, off_decay:off_q_b], pad], axis=2)
    main = jnp.concatenate([w_in[:, :, :off_decay], w_in[:, :, off_q_b:off_merge + 2 * D]], axis=2)
    return main.astype(BF16), small.astype(BF16)


def _pad_lanes(v):
    depth, k = v.shape
    return jnp.concatenate([v, jnp.zeros((depth, LANES - k), v.dtype)], axis=1).reshape(depth, 1, LANES)


def kernel(x_prompt, x_sample, state_conv, state_delta, state_hgrn, c_prompt, c_sample, ln0_g, ln0_b, w_ada, b_ada, w_in, conv_w, a_log, dt_bias, norm_a, lb_param, norm_b, w_br_a, w_br_b, w_o, ln1_g, ln1_b, w_grp, b_grp, w_rt, b_rt, w_gate_e, w_up_e, w_down_e, ln2_g, ln2_b):
    depth = w_in.shape[0]
    alpha = (2 * depth) ** 0.25
    bp, tp, _ = x_prompt.shape
    bs, ts, _ = x_sample.shape
    n_p = bp * tp
    tsp = SUBLANES
    assert ts <= tsp
    n_s = bs * tsp

    w_main, w_small = _pack_w_in(w_in)
    w_a16, w_b16, w_o16 = w_br_a.astype(BF16), w_br_b.astype(BF16), w_o.astype(BF16)
    w_r = jnp.concatenate([w_grp, w_rt, jnp.zeros((depth, D, LANES - N_GROUPS - N_EXP), F32)], axis=2)
    b_r = jnp.concatenate([b_grp, b_rt, jnp.zeros((depth, LANES - N_GROUPS - N_EXP), F32)], axis=1)
    b_r = b_r.reshape(depth, 1, LANES)
    a_log_p, dt_bias_p = _pad_lanes(a_log), _pad_lanes(dt_bias)
    norm_a3, norm_b3 = norm_a.reshape(depth, 1, HD), norm_b.reshape(depth, 1, HD)
    ln1_g3, ln1_b3 = ln1_g.reshape(depth, 1, D), ln1_b.reshape(depth, 1, D)
    ln2_g3, ln2_b3 = ln2_g.reshape(depth, 1, D), ln2_b.reshape(depth, 1, D)

    mods = _modulation(jnp.concatenate([c_prompt, c_sample], axis=0), w_ada, b_ada)
    mods_p = mods[:, :bp].reshape(depth, bp, 1, 6 * D)
    mods_s = jnp.repeat(mods[:, bp:], tsp, axis=1)

    xs_pad = jnp.concatenate([x_sample, jnp.zeros((bs, tsp - ts, D), x_sample.dtype)], axis=1)
    cp = math.gcd(tp, 128)
    tm_p = math.gcd(tp, 512)
    tm_s = math.gcd(n_s, 512)
    bb_s = math.gcd(bs, 8)
    state_shape = lambda nb: (depth, nb, NH, HD, HD)
    conv_shape = lambda nb: (depth, nb, CONV_W - 1, CONV_DIM)
    groups = [
        dict(mode="row", n=n_p, nb=bp, t=tp, C=cp, tv=cp, tm=tm_p, tpb=tp // tm_p, bb=1, mods=mods_p,
             conv=jnp.zeros(conv_shape(bp), F32), delta=jnp.zeros(state_shape(bp), F32),
             hgrn=jnp.zeros(state_shape(bp), F32)),
        dict(mode="tok", n=n_s, nb=bs, t=tsp, C=tsp, tv=ts, tm=tm_s, tpb=1, bb=bb_s, mods=mods_s,
             conv=state_conv, delta=state_delta, hgrn=state_hgrn),
    ]
    for g in groups:
        g["tm_dma"] = math.gcd(g["t"] if g["mode"] == "row" else g["n"], 1024)
    for g in groups:
        g["conv_o"] = jnp.zeros(conv_shape(g["nb"]), F32)
        g["delta_o"] = jnp.zeros(state_shape(g["nb"]), F32)
        g["hgrn_o"] = jnp.zeros(state_shape(g["nb"]), F32)
    xs = [_ln0(x_prompt.reshape(n_p, D), ln0_g, ln0_b, tm_p),
          _ln0(xs_pad.reshape(n_s, D), ln0_g, ln0_b, tm_s)]

    for l in range(depth):
        x1s, h2s, rws, res = [], [], [], []
        for gi, g in enumerate(groups):
            tm_in = math.gcd(g["t"] if g["mode"] == "row" else g["n"], 1024)
            if g["mode"] == "row":
                proj, small = _inproj(xs[gi], g["mods"], w_main, w_small, l, g["mode"], tm_in,
                                      g["t"] // tm_in)
                proj3 = proj.reshape(g["nb"], g["t"], NPROJ)
                small3 = small.reshape(g["nb"], g["t"], LANES)
            else:
                proj3, small3 = _inproj(xs[gi], g["mods"], w_main, w_small, l, g["mode"], tm_in, 1,
                                        t3=g["t"])
                proj = proj3
            o_a, g["conv_o"], g["delta_o"] = _mixer_a(
                proj3, small3, g["conv"], g["delta"], g["conv_o"], g["delta_o"], conv_w, a_log_p, dt_bias_p,
                norm_a3, l, g["C"], g["tv"], g["bb"])
            o_b, g["hgrn_o"] = _mixer_b(proj3, g["hgrn"], g["hgrn_o"], lb_param, norm_b3, l, g["C"],
                                        g["tv"], g["bb"])
            halve = 1 if g["mode"] == "row" else 2
            if g["mode"] == "row":
                o_a, o_b = o_a.reshape(g["n"], D), o_b.reshape(g["n"], D)
            x1, h2, rw, re = _post_mixer(o_a, o_b, proj, xs[gi],
                                         g["mods"], w_a16, w_b16, w_o16, ln1_g3, ln1_b3, w_r, b_r, l,
                                         g["mode"], g["tm"] // halve, g["tpb"] * halve, alpha)
            x1s.append(x1)
            h2s.append(h2)
            rws.append(rw)
            res.append(re)
        eid = jnp.concatenate([r[:, :2] for r in res], axis=0)
        n_rows, block_e, n_used, d0, d1 = _route_meta(eid)
        if l == 0:
            xs_sorted = jnp.zeros((n_rows, D), F32)
        off = 0
        for gi, g in enumerate(groups):
            sl = slice(off, off + g["n"])
            xs_sorted = _dispatch(h2s[gi], d0[sl], d1[sl], xs_sorted, g["tm_dma"])
            off += g["n"]
        ys = _experts(xs_sorted, block_e, n_used, w_gate_e, w_up_e, w_down_e, l)
        off = 0
        for gi, g in enumerate(groups):
            sl = slice(off, off + g["n"])
            xs[gi] = _combine(ys, d0[sl], d1[sl], rws[gi], x1s[gi], g["mods"], ln2_g3, ln2_b3, l,
                              g["mode"], g["tm_dma"], g["t"] // g["tm_dma"], alpha)
            off += g["n"]

    y_prompt = xs[0].reshape(bp, tp, D)
    y_sample = xs[1].reshape(bs, tsp, D)[:, :ts]
    gp, gs = groups
    return (y_prompt, y_sample, gp["conv_o"], gp["delta_o"], gp["hgrn_o"],
            gs["conv_o"], gs["delta_o"], gs["hgrn_o"])
```
